```python
import jax, jax.numpy as jnp
from jax import lax
import numpy as np

D_MODEL = 4096
BATCH = 8
SEQ = 2048
DEPTH = 4

GRID_W = 64
CTX_LEN = 256
N_MIXERS = 2
EPS = 1e-6
ADA_RANK = 512
N_MOD = 6
CONV_W = 3
GLA_HEADS = 8
GLA_QK = D_MODEL // 2
GLA_V = D_MODEL
GLA_DK = GLA_QK // GLA_HEADS
GLA_DV = GLA_V // GLA_HEADS
GLA_GATE_RANK = 16
GLA_GATE_TAU = 16.0
GLA_LOG_DECAY_FLOOR = -1.0
GLA_CHUNK = 64
GLA_IN = 2 * GLA_QK + 2 * GLA_V + 2 * GLA_GATE_RANK
N_EXPERTS = 64
N_GROUPS = 8
TOPK_GROUPS = 4
TOP_K = 8
D_EXPERT = 128
D_SHARED = 128
ROUTED_SCALE = 2.5

kernel_name = "hybrid_conv_gla_moe_dit"


def rmsnorm(x, w):
    xf = x.astype(jnp.float32)
    y = xf * lax.rsqrt(jnp.mean(xf * xf, axis=-1, keepdims=True) + EPS)
    return (y * w.astype(jnp.float32)).astype(x.dtype)


def modulate(x, w, shift, scale):
    return rmsnorm(x, w) * (1 + scale) + shift


def ada_mod(cond, w_down, w_up, b, n):
    m = (jax.nn.silu(cond) @ w_down) @ w_up[:, :n * D_MODEL] + b[:n * D_MODEL]
    return jnp.split(m, n, axis=-1)


def conv3_centred(u, w):
    L = u.shape[-2]
    up = jnp.pad(u, [(0, 0)] * (u.ndim - 2) + [(1, 1), (0, 0)])
    return up[..., :L, :] * w[0] + up[..., 1:L + 1, :] * w[1] + up[..., 2:, :] * w[2]


def short_conv_mixer(h, w_in, w_conv, w_out, grid):
    bg, cg, v = jnp.split(h @ w_in, 3, axis=-1)
    u = cg * v
    if grid:
        b_, L, d = u.shape
        rows = L // GRID_W
        u = conv3_centred(u.reshape(b_, rows, GRID_W, d), w_conv).reshape(b_, L, d)
    else:
        u = conv3_centred(u, w_conv)
    return (bg * u) @ w_out


def gla_project(h, w_in, w_a2f, b_a2f, w_a2b, b_a2b, with_query):
    b_, L, _ = h.shape

    def heads(t, d):
        return t.reshape(b_, L, GLA_HEADS, d).transpose(0, 2, 1, 3).astype(jnp.float32)

    def log_decay(a, w, b):
        g = jax.nn.log_sigmoid((a @ w + b).astype(jnp.float32)) / GLA_GATE_TAU
        return heads(jnp.maximum(g, GLA_LOG_DECAY_FLOOR), GLA_DK)

    if with_query:
        q, k, v, r, a = jnp.split(h @ w_in, [GLA_QK, 2 * GLA_QK, 2 * GLA_QK + GLA_V,
                                             2 * GLA_QK + 2 * GLA_V], axis=-1)
        q = heads(q, GLA_DK) * (GLA_DK ** -0.5)
    else:
        k, v = jnp.split(h @ w_in[:, GLA_QK:2 * GLA_QK + GLA_V], [GLA_QK], axis=-1)
        a = h @ w_in[:, 2 * GLA_QK + 2 * GLA_V:]
        q, r = None, None
    a_f, a_b = jnp.split(a, 2, axis=-1)
    return (q, heads(k, GLA_DK), heads(v, GLA_DV),
            log_decay(a_f, w_a2f, b_a2f), log_decay(a_b, w_a2b, b_a2b), r)


def gla_chunked(q, k, v, g, s0, with_output):
    b_, H, L, _ = k.shape
    n = L // GLA_CHUNK

    def chunks(t):
        return jnp.moveaxis(t.reshape(b_, H, n, GLA_CHUNK, t.shape[-1]), 2, 0)

    kc, vc, gc = chunks(k), chunks(v), chunks(g)
    bcum = jnp.cumsum(gc, axis=-2)
    b_last = bcum[..., -1:, :]
    k_end = kc * jnp.exp(b_last - bcum)
    decay = jnp.exp(b_last[..., 0, :])[..., None]
    if not with_output:
        def step_state(s, xs):
            ke, vv, dc = xs
            return dc * s + jnp.einsum('bhck,bhcv->bhkv', ke, vv), None
        s_fin, _ = lax.scan(step_state, s0, (k_end, vc, decay))
        return None, s_fin
    qc = chunks(q)
    q_dec = qc * jnp.exp(bcum)
    k_inv = kc * jnp.exp(-bcum)
    mask = jnp.tril(jnp.ones((GLA_CHUNK, GLA_CHUNK), jnp.float32))
    att = jnp.einsum('nbhck,nbhsk->nbhcs', q_dec, k_inv) * mask
    o_intra = jnp.einsum('nbhcs,nbhsv->nbhcv', att, vc)

    def step(s, xs):
        qd, ke, vv, dc = xs
        o_inter = jnp.einsum('bhck,bhkv->bhcv', qd, s)
        return dc * s + jnp.einsum('bhck,bhcv->bhkv', ke, vv), o_inter

    s_fin, o_inter = lax.scan(step, s0, (q_dec, k_end, vc, decay))
    o = jnp.moveaxis(o_intra + o_inter, 0, 2).reshape(b_, H, L, -1)
    return o, s_fin


def gla_output(o, r, norm_w, w_out, dtype):
    o = o * lax.rsqrt(jnp.mean(o * o, axis=-1, keepdims=True) + EPS) * norm_w.astype(jnp.float32)
    b_, H, L, dv = o.shape
    o = o.transpose(0, 2, 1, 3).reshape(b_, L, H * dv).astype(dtype)
    return (o * jax.nn.silu(r)) @ w_out


def gla_mixer(hl, hc, w_in, w_a2f, b_a2f, w_a2b, b_a2b, norm_w, w_out, ctx_out):
    flip = lambda t: jnp.flip(t, axis=2)
    ql, kl, vl, gfl, gbl, rl = gla_project(hl, w_in, w_a2f, b_a2f, w_a2b, b_a2b, True)
    qc, kc, vc, gfc, gbc, rc = gla_project(hc, w_in, w_a2f, b_a2f, w_a2b, b_a2b, ctx_out)
    s0 = jnp.zeros((hl.shape[0], GLA_HEADS, GLA_DK, GLA_DV), jnp.float32)
    oc_f, sc_f = gla_chunked(qc, kc, vc, gfc, s0, ctx_out)
    ol_f, _ = gla_chunked(ql, kl, vl, gfl, sc_f, True)
    qc_b = flip(qc) if ctx_out else None
    oc_b, sc_b = gla_chunked(qc_b, flip(kc), flip(vc), flip(gbc), s0, ctx_out)
    ol_b, _ = gla_chunked(flip(ql), flip(kl), flip(vl), flip(gbl), sc_b, True)
    yl = gla_output(ol_f + flip(ol_b), rl, norm_w, w_out, hl.dtype)
    yc = gla_output(oc_f + flip(oc_b), rc, norm_w, w_out, hc.dtype) if ctx_out else None
    return yl, yc


def moe(h, router_w, router_bias, w_in, w_out, ws_in, ws_out):
    shp = h.shape
    t = h.reshape(-1, D_MODEL)
    scores = jax.nn.sigmoid((t @ router_w).astype(jnp.float32))
    sel = scores + router_bias.astype(jnp.float32)
    grp = sel.reshape(-1, N_GROUPS, N_EXPERTS // N_GROUPS)
    grp_score = jnp.sum(lax.top_k(grp, 2)[0], axis=-1)
    _, gidx = lax.top_k(grp_score, TOPK_GROUPS)
    gmask = jnp.sum(jax.nn.one_hot(gidx, N_GROUPS, dtype=jnp.float32), axis=-2)
    emask = jnp.repeat(gmask, N_EXPERTS // N_GROUPS, axis=-1)
    sel = jnp.where(emask > 0, sel, -jnp.inf)
    _, eidx = lax.top_k(sel, TOP_K)
    w = jnp.take_along_axis(scores, eidx, axis=-1)
    w = w / jnp.sum(w, axis=-1, keepdims=True) * ROUTED_SCALE
    gates = jnp.sum(jax.nn.one_hot(eidx, N_EXPERTS, dtype=jnp.float32) * w[..., None], axis=-2)
    hu = jnp.einsum('td,edf->tef', t, w_in)
    gt, up = jnp.split(hu, 2, axis=-1)
    act = jax.nn.silu(gt) * up * gates[..., None].astype(t.dtype)
    routed = jnp.einsum('tef,efd->td', act, w_out)
    sg, su = jnp.split(t @ ws_in, 2, axis=-1)
    shared = (jax.nn.silu(sg) * su) @ ws_out
    return (routed + shared).reshape(shp)


def setup_inputs(seed: int = 0) -> dict:
    key = jax.random.key(seed)
    ks = jax.random.split(key, 26)
    nrm = lambda k, shape, s: jax.random.normal(k, shape, jnp.float32) * s
    n_a = (DEPTH + 1) // 2
    n_b = DEPTH // 2
    D = D_MODEL
    return {
        "x": nrm(ks[0], (BATCH, SEQ, D), 1.0),
        "c": nrm(ks[1], (BATCH, D), 1.0),
        "ctx": nrm(ks[2], (BATCH, CTX_LEN, D), 1.0),
        "c_ctx": nrm(ks[3], (D,), 1.0),
        "ada_w_down": nrm(ks[4], (DEPTH, D, ADA_RANK), D ** -0.5),
        "ada_w_up": nrm(ks[5], (DEPTH, ADA_RANK, N_MOD * D), 0.3 * ADA_RANK ** -0.5),
        "ada_b": nrm(ks[6], (DEPTH, N_MOD * D), 0.01),
        "norm_mix": 1.0 + nrm(ks[7], (DEPTH, D), 0.01),
        "norm_ffn": 1.0 + nrm(ks[8], (DEPTH, D), 0.01),
        "conv_w_in": nrm(ks[9], (n_a, D, 3 * D), D ** -0.5),
        "conv_w": nrm(ks[10], (n_a, CONV_W, D), CONV_W ** -0.5),
        "conv_w_out": nrm(ks[11], (n_a, D, D), D ** -0.5),
        "gla_w_in": nrm(ks[12], (n_b, D, GLA_IN), D ** -0.5),
        "gla_w_a2_fwd": nrm(ks[13], (n_b, GLA_GATE_RANK, GLA_QK), GLA_GATE_RANK ** -0.5),
        "gla_b_a2_fwd": nrm(ks[14], (n_b, GLA_QK), 0.1),
        "gla_w_a2_bwd": nrm(ks[15], (n_b, GLA_GATE_RANK, GLA_QK), GLA_GATE_RANK ** -0.5),
        "gla_b_a2_bwd": nrm(ks[16], (n_b, GLA_QK), 0.1),
        "gla_norm": 1.0 + nrm(ks[17], (n_b, GLA_DV), 0.01),
        "gla_w_out": nrm(ks[18], (n_b, D, D), D ** -0.5),
        "router_w": nrm(ks[19], (DEPTH, D, N_EXPERTS), D ** -0.5),
        "router_bias": nrm(ks[20], (DEPTH, N_EXPERTS), 0.01),
        "exp_w_in": nrm(ks[21], (DEPTH, N_EXPERTS, D, 2 * D_EXPERT), D ** -0.5),
        "exp_w_out": nrm(ks[22], (DEPTH, N_EXPERTS, D_EXPERT, D), D_EXPERT ** -0.5),
        "shared_w_in": nrm(ks[23], (DEPTH, D, 2 * D_SHARED), D ** -0.5),
        "shared_w_out": nrm(ks[24], (DEPTH, D_SHARED, D), D_SHARED ** -0.5),
        "norm_final": 1.0 + nrm(ks[25], (D,), 0.01),
    }


def reference(x, c, ctx, c_ctx, ada_w_down, ada_w_up, ada_b, norm_mix, norm_ffn,
              conv_w_in, conv_w, conv_w_out, gla_w_in, gla_w_a2_fwd, gla_b_a2_fwd,
              gla_w_a2_bwd, gla_b_a2_bwd, gla_norm, gla_w_out, router_w, router_bias,
              exp_w_in, exp_w_out, shared_w_in, shared_w_out, norm_final):
    xl, xc = x, ctx
    for i in range(DEPTH):
        kind = i % N_MIXERS
        j = i // N_MIXERS
        last = i == DEPTH - 1
        ctx_needed = (not last) or kind == 1
        ml = ada_mod(c, ada_w_down[i], ada_w_up[i], ada_b[i], N_MOD)
        sh_a, sc_a, g_a, sh_f, sc_f, g_f = [m[:, None, :] for m in ml]
        hl = modulate(xl, norm_mix[i], sh_a, sc_a)
        if ctx_needed:
            mc = ada_mod(c_ctx, ada_w_down[i], ada_w_up[i], ada_b[i], 2 if last else N_MOD)
            hc = modulate(xc, norm_mix[i], mc[0], mc[1])
        if kind == 0:
            yl = short_conv_mixer(hl, conv_w_in[j], conv_w[j], conv_w_out[j], True)
            yc = None if last else short_conv_mixer(hc, conv_w_in[j], conv_w[j], conv_w_out[j], False)
        else:
            yl, yc = gla_mixer(hl, hc, gla_w_in[j], gla_w_a2_fwd[j], gla_b_a2_fwd[j],
                               gla_w_a2_bwd[j], gla_b_a2_bwd[j], gla_norm[j], gla_w_out[j],
                               not last)
        moe_w = (router_w[i], router_bias[i], exp_w_in[i], exp_w_out[i],
                 shared_w_in[i], shared_w_out[i])
        xl = xl + g_a * yl
        xl = xl + g_f * moe(modulate(xl, norm_ffn[i], sh_f, sc_f), *moe_w)
        if not last:
            xc = xc + mc[2] * yc
            xc = xc + mc[5] * moe(modulate(xc, norm_ffn[i], mc[3], mc[4]), *moe_w)
    return rmsnorm(xl, norm_final)
```

```python
import functools

import jax
import jax.numpy as jnp
from jax import lax
from jax.experimental import pallas as pl
from jax.experimental.pallas import tpu as pltpu

GRID_W = 64
EPS = 1e-6
N_MOD = 6
GLA_HEADS = 8
GLA_GATE_RANK = 16
GLA_GATE_TAU = 16.0
GLA_LOG_DECAY_FLOOR = -1.0
N_EXPERTS = 64
N_GROUPS = 8
TOPK_GROUPS = 4
TOP_K = 8
ROUTED_SCALE = 2.5

LANES = 128
GLA_ROWS = 128
TILES_PER_SAMPLE = 4
MOD_ROWS = 16
VMEM_LIMIT_BYTES = 56 * 1024 * 1024

BF16 = jnp.bfloat16
F32 = jnp.float32
NEG_INF = float("-inf")


def _params(*sem):
    return pltpu.CompilerParams(dimension_semantics=sem, vmem_limit_bytes=VMEM_LIMIT_BYTES)


def _pick(n, prefs):
    for p in prefs:
        if n % p == 0:
            return p
    return n


def _ada_kernel(cond_ref, wd_ref, wu_ref, b_ref, o_ref, z_ref):
    @pl.when(pl.program_id(1) == 0)
    def _():
        s = jax.nn.silu(cond_ref[...]).astype(BF16)
        z_ref[...] = jnp.dot(s, wd_ref[0].astype(BF16), preferred_element_type=F32)

    o_ref[0] = jnp.dot(z_ref[...].astype(BF16), wu_ref[0].astype(BF16),
                       preferred_element_type=F32) + b_ref[0]


def _ada_all(cond, w_down, w_up, b):
    depth, d, r = w_down.shape
    n = w_up.shape[-1]
    tn = _pick(n, (2048, 1024, 512, 256, 128))
    return pl.pallas_call(
        _ada_kernel,
        grid=(depth, n // tn),
        in_specs=[
            pl.BlockSpec((MOD_ROWS, d), lambda l, j: (0, 0)),
            pl.BlockSpec((1, d, r), lambda l, j: (l, 0, 0)),
            pl.BlockSpec((1, r, tn), lambda l, j: (l, 0, j)),
            pl.BlockSpec((1, 1, tn), lambda l, j: (l, 0, j)),
        ],
        out_specs=pl.BlockSpec((1, MOD_ROWS, tn), lambda l, j: (l, 0, j)),
        out_shape=jax.ShapeDtypeStruct((depth, MOD_ROWS, n), F32),
        scratch_shapes=[pltpu.VMEM((MOD_ROWS, r), F32)],
        compiler_params=_params("arbitrary", "arbitrary"),
        name="ada_mod",
    )(cond, w_down, w_up, b.reshape(depth, 1, n))


class _Layout:
    def __init__(self, batch, ctx_len, seq, d):
        self.batch, self.ctx_len, self.seq, self.d = batch, ctx_len, seq, d
        self.p = ctx_len + seq
        self.t = batch * self.p
        assert self.p % TILES_PER_SAMPLE == 0
        self.tm = self.p // TILES_PER_SAMPLE
        assert GRID_W & (GRID_W - 1) == 0
        assert self.tm % GRID_W == 0 and ctx_len % GRID_W == 0 and self.tm % 16 == 0
        assert ctx_len <= self.tm and batch < MOD_ROWS
        self.n_tiles = self.t // self.tm
        self.slab = _pick(self.tm, (32, 16))
        assert ctx_len % self.slab == 0


def _mod_specs(lay, idx, two_axes=True):
    d = lay.d
    if two_axes:
        lat = pl.BlockSpec((1, 1, d), lambda i, j: (i // TILES_PER_SAMPLE, 0, idx))
        ctx = pl.BlockSpec((1, 1, d), lambda i, j: (lay.batch, 0, idx))
    else:
        lat = pl.BlockSpec((1, 1, d), lambda i: (i // TILES_PER_SAMPLE, 0, idx))
        ctx = pl.BlockSpec((1, 1, d), lambda i: (lay.batch, 0, idx))
    return lat, ctx


def _tile_row0(lay):
    return (pl.program_id(0) % TILES_PER_SAMPLE) * lay.tm


def _row_in_sample(lay, shape):
    return _tile_row0(lay) + lax.broadcasted_iota(jnp.int32, shape, 0)


def _modnorm_rows(lay, x_ref, nw_ref, shl_ref, scl_ref, shc_ref, scc_ref, h_ref):
    nw = nw_ref[...]
    slab = lay.slab
    row0 = _tile_row0(lay)

    def body(s, carry):
        r = pl.multiple_of(s * slab, slab)
        x = x_ref[pl.ds(r, slab), :]
        ms = jnp.mean(x * x, axis=-1, keepdims=True)
        y = x * lax.rsqrt(ms + EPS) * nw
        is_ctx = row0 + r < lay.ctx_len
        sc = jnp.where(is_ctx, scc_ref[0], scl_ref[0])
        sh = jnp.where(is_ctx, shc_ref[0], shl_ref[0])
        h_ref[pl.ds(r, slab), :] = (y * (1.0 + sc) + sh).astype(h_ref.dtype)
        return carry

    lax.fori_loop(0, lay.tm // slab, body, 0)


def _proj_kernel(x_ref, nw_ref, shl, scl, shc, scc, w_ref, o_ref, h_ref, *, lay):
    @pl.when(pl.program_id(1) == 0)
    def _():
        _modnorm_rows(lay, x_ref, nw_ref, shl, scl, shc, scc, h_ref)

    o_ref[...] = jnp.dot(h_ref[...], w_ref[...], preferred_element_type=F32).astype(o_ref.dtype)


def _proj_aux_kernel(x_ref, nw_ref, shl, scl, shc, scc, w_ref, wa_ref, o_ref, a_ref, h_ref, *, lay):
    @pl.when(pl.program_id(1) == 0)
    def _():
        _modnorm_rows(lay, x_ref, nw_ref, shl, scl, shc, scc, h_ref)
        a_ref[...] = jnp.dot(h_ref[...], wa_ref[...], preferred_element_type=F32)

    o_ref[...] = jnp.dot(h_ref[...], w_ref[...], preferred_element_type=F32).astype(o_ref.dtype)


def _conv_proj_kernel(x_ref, nw_ref, shl, scl, shc, scc, w_ref, cw_ref, o_ref, h_ref, *, lay, tc):
    @pl.when(pl.program_id(1) == 0)
    def _():
        _modnorm_rows(lay, x_ref, nw_ref, shl, scl, shc, scc, h_ref)

    tm = lay.tm
    r = jnp.dot(h_ref[...], w_ref[...], preferred_element_type=F32)
    bg, cg, v = r[:, :tc], r[:, tc:2 * tc], r[:, 2 * tc:]
    u = cg * v
    row = _row_in_sample(lay, (tm, tc))
    is_ctx = row < lay.ctx_len
    col = (row - lay.ctx_len) & (GRID_W - 1)
    first = (is_ctx & (row == 0)) | (~is_ctx & (col == 0))
    last = (is_ctx & (row == lay.ctx_len - 1)) | (~is_ctx & (col == GRID_W - 1))
    prev = jnp.where(first, 0.0, pltpu.roll(u, 1, axis=0))
    nxt = jnp.where(last, 0.0, pltpu.roll(u, tm - 1, axis=0))
    cw = cw_ref[...]
    conv = prev * cw[0:1, :] + u * cw[1:2, :] + nxt * cw[2:3, :]
    o_ref[...] = (bg * conv).astype(o_ref.dtype)


def _norm_proj(lay, x, nw, mods, shift_idx, scale_idx, w, *, out_dtype, tn, w_aux=None,
               conv_w=None, name):
    d, n = w.shape
    shl, shc = _mod_specs(lay, shift_idx)
    scl, scc = _mod_specs(lay, scale_idx)
    in_specs = [
        pl.BlockSpec((lay.tm, d), lambda i, j: (i, 0)),
        pl.BlockSpec((1, d), lambda i, j: (0, 0)),
        shl, scl, shc, scc,
        pl.BlockSpec((d, tn), lambda i, j: (0, j)),
    ]
    args = [x, nw, mods, mods, mods, mods, w]
    scratch = [pltpu.VMEM((lay.tm, d), BF16)]
    params = _params("arbitrary", "arbitrary")
    if conv_w is not None:
        tc = tn // 3
        in_specs.append(pl.BlockSpec((3, tc), lambda i, j: (0, j)))
        return pl.pallas_call(
            functools.partial(_conv_proj_kernel, lay=lay, tc=tc),
            grid=(lay.n_tiles, n // tn), in_specs=in_specs,
            out_specs=pl.BlockSpec((lay.tm, tc), lambda i, j: (i, j)),
            out_shape=jax.ShapeDtypeStruct((lay.t, n // 3), out_dtype),
            scratch_shapes=scratch, compiler_params=params, name=name,
        )(*args, conv_w)
    if w_aux is not None:
        na = w_aux.shape[1]
        in_specs.append(pl.BlockSpec((d, na), lambda i, j: (0, 0)))
        return pl.pallas_call(
            functools.partial(_proj_aux_kernel, lay=lay),
            grid=(lay.n_tiles, n // tn), in_specs=in_specs,
            out_specs=[pl.BlockSpec((lay.tm, tn), lambda i, j: (i, j)),
                       pl.BlockSpec((lay.tm, na), lambda i, j: (i, 0))],
            out_shape=[jax.ShapeDtypeStruct((lay.t, n), out_dtype),
                       jax.ShapeDtypeStruct((lay.t, na), F32)],
            scratch_shapes=scratch, compiler_params=params, name=name,
        )(*args, w_aux)
    return pl.pallas_call(
        functools.partial(_proj_kernel, lay=lay),
        grid=(lay.n_tiles, n // tn), in_specs=in_specs,
        out_specs=pl.BlockSpec((lay.tm, tn), lambda i, j: (i, j)),
        out_shape=jax.ShapeDtypeStruct((lay.t, n), out_dtype),
        scratch_shapes=scratch, compiler_params=params, name=name,
    )(*args)


def _out_proj_kernel(a_ref, w_ref, x_ref, gl_ref, gc_ref, o_ref, *, lay):
    y = jnp.dot(a_ref[...], w_ref[...], preferred_element_type=F32)
    is_ctx = _row_in_sample(lay, y.shape) < lay.ctx_len
    gate = jnp.where(is_ctx, gc_ref[0], gl_ref[0])
    o_ref[...] = x_ref[...] + gate * y


def _out_proj(lay, a, w, x, mods, gate_idx, *, tn, name):
    k, d = w.shape
    gl = pl.BlockSpec((1, 1, tn), lambda i, j: (i // TILES_PER_SAMPLE, 0, gate_idx * (d // tn) + j))
    gc = pl.BlockSpec((1, 1, tn), lambda i, j: (lay.batch, 0, gate_idx * (d // tn) + j))
    return pl.pallas_call(
        functools.partial(_out_proj_kernel, lay=lay),
        grid=(lay.n_tiles, d // tn),
        in_specs=[
            pl.BlockSpec((lay.tm, k), lambda i, j: (i, 0)),
            pl.BlockSpec((k, tn), lambda i, j: (0, j)),
            pl.BlockSpec((lay.tm, tn), lambda i, j: (i, j)),
            gl, gc,
        ],
        out_specs=pl.BlockSpec((lay.tm, tn), lambda i, j: (i, j)),
        out_shape=jax.ShapeDtypeStruct((lay.t, d), F32),
        input_output_aliases={2: 0},
        compiler_params=_params("arbitrary", "arbitrary"),
        name=name,
    )(a, w, x, mods, mods)


def _gla_kernel(q_ref, k_ref, v_ref, r_ref, a_ref, waf_ref, baf_ref, wab_ref, bab_ref, nw_ref,
                y_ref, of_ref, s_ref, *, n_chunks, n_ctx_chunks, dk):
    c_rows = GLA_ROWS
    scale = dk ** -0.5
    ri = lax.broadcasted_iota(jnp.int32, (c_rows, c_rows), 0)
    ci = lax.broadcasted_iota(jnp.int32, (c_rows, c_rows), 1)
    nt = (((1,), (1,)), ((), ()))
    tn_dims = (((0,), (0,)), ((), ()))

    def chunk(c, forward):
        rows = pl.ds(pl.multiple_of(c * c_rows, c_rows), c_rows)
        keep = (ci <= ri) if forward else (ci >= ri)
        tri = keep.astype(BF16)
        wa, ba = (waf_ref, baf_ref) if forward else (wab_ref, bab_ref)
        z = jnp.dot(a_ref[rows, :].astype(BF16), wa[...], preferred_element_type=F32) + ba[...]
        g = jnp.maximum(jax.nn.log_sigmoid(z) / GLA_GATE_TAU, GLA_LOG_DECAY_FLOOR)
        g1 = g.astype(BF16)
        rem = g - g1.astype(F32)
        g2 = rem.astype(BF16)
        g3 = (rem - g2.astype(F32)).astype(BF16)
        bcum = (jnp.dot(tri, g1, preferred_element_type=F32)
                + jnp.dot(tri, g2, preferred_element_type=F32)
                + jnp.dot(tri, g3, preferred_element_type=F32))
        mid = bcum[c_rows // 2:c_rows // 2 + 1, :]
        tot = bcum[c_rows - 1:c_rows, :] if forward else bcum[0:1, :]
        q = q_ref[rows, :].astype(F32) * scale
        k = k_ref[rows, :].astype(F32)
        v = v_ref[rows, :]
        q_rel = (q * jnp.exp(bcum - mid)).astype(BF16)
        k_rel = (k * jnp.exp(mid - bcum)).astype(BF16)
        q_dec = (q * jnp.exp(bcum)).astype(BF16)
        k_end = (k * jnp.exp(tot - bcum)).astype(BF16)
        att = lax.dot_general(q_rel, k_rel, nt, preferred_element_type=F32)
        att = jnp.where(keep, att, 0.0).astype(BF16)
        s = s_ref[...]
        o = (jnp.dot(att, v, preferred_element_type=F32)
             + lax.dot_general(q_dec, s.astype(BF16), nt, preferred_element_type=F32))
        s_ref[...] = jnp.exp(tot) * s + lax.dot_general(v, k_end, tn_dims, preferred_element_type=F32)
        return rows, o

    s_ref[...] = jnp.zeros_like(s_ref)

    def fwd_body(n, carry):
        rows, o = chunk(n, True)
        of_ref[rows, :] = o
        return carry

    lax.fori_loop(0, n_chunks, fwd_body, 0)

    s_ref[...] = jnp.zeros_like(s_ref)
    nw = nw_ref[...]

    def bwd_body(n, carry):
        c = jnp.where(n < n_ctx_chunks, n_ctx_chunks - 1 - n, n_chunks - 1 - (n - n_ctx_chunks))
        rows, o_b = chunk(c, False)
        o = of_ref[rows, :] + o_b
        o = o * lax.rsqrt(jnp.mean(o * o, axis=-1, keepdims=True) + EPS) * nw
        y_ref[rows, :] = (o * jax.nn.silu(r_ref[rows, :].astype(F32))).astype(y_ref.dtype)
        return carry

    lax.fori_loop(0, n_chunks, bwd_body, 0)


def _gla(lay, qkvr, a, waf, baf, wab, bab, norm_w):
    d = lay.d
    dk = d // 2 // GLA_HEADS
    dv = d // GLA_HEADS
    assert lay.ctx_len % GLA_ROWS == 0 and lay.seq % GLA_ROWS == 0
    n_chunks = lay.p // GLA_ROWS
    kern = functools.partial(_gla_kernel, n_chunks=n_chunks,
                             n_ctx_chunks=lay.ctx_len // GLA_ROWS, dk=dk)
    h = GLA_HEADS
    return pl.pallas_call(
        kern,
        grid=(lay.batch, h),
        in_specs=[
            pl.BlockSpec((lay.p, dk), lambda b, i: (b, i)),
            pl.BlockSpec((lay.p, dk), lambda b, i: (b, h + i)),
            pl.BlockSpec((lay.p, dv), lambda b, i: (b, h + i)),
            pl.BlockSpec((lay.p, dv), lambda b, i: (b, 2 * h + i)),
            pl.BlockSpec((lay.p, LANES), lambda b, i: (b, 0)),
            pl.BlockSpec((LANES, dk), lambda b, i: (0, i)),
            pl.BlockSpec((1, dk), lambda b, i: (0, i)),
            pl.BlockSpec((LANES, dk), lambda b, i: (0, i)),
            pl.BlockSpec((1, dk), lambda b, i: (0, i)),
            pl.BlockSpec((1, dv), lambda b, i: (0, 0)),
        ],
        out_specs=pl.BlockSpec((lay.p, dv), lambda b, i: (b, i)),
        out_shape=jax.ShapeDtypeStruct((lay.t, d), BF16),
        scratch_shapes=[pltpu.VMEM((lay.p, dv), F32), pltpu.VMEM((dv, dk), F32)],
        compiler_params=_params("arbitrary", "arbitrary"),
        name="gla_scan",
    )(qkvr, qkvr, qkvr, qkvr, a, waf, baf, wab, bab, norm_w)


def _route(logits, bias):
    n_rows = logits.shape[0]
    per_group = N_EXPERTS // N_GROUPS
    lane_i = lax.broadcasted_iota(jnp.int32, (n_rows, LANES), 1)
    lane = lane_i.astype(F32)
    group = (lane_i // per_group).astype(F32)
    valid = lane_i < N_EXPERTS
    big = float(2 * LANES)
    scores = jax.nn.sigmoid(logits)
    sel = jnp.where(valid, scores + bias, NEG_INF)

    def partner(x, s):
        return jnp.where((lane_i & s) == 0, pltpu.roll(x, LANES - s, axis=1), pltpu.roll(x, s, axis=1))

    def group_reduce(x, op):
        s = 1
        while s < per_group:
            x = op(x, partner(x, s))
            s *= 2
        return x

    m1 = group_reduce(sel, jnp.maximum)
    first = group_reduce(jnp.where(sel == m1, lane, big), jnp.minimum)
    m2 = group_reduce(jnp.where(lane == first, NEG_INF, sel), jnp.maximum)
    cur = jnp.where(valid, m1 + m2, NEG_INF)

    def pick_best(cur, ident):
        m = jnp.max(cur, axis=1, keepdims=True)
        best = jnp.min(jnp.where(cur == m, ident, big), axis=1, keepdims=True)
        return ident == best

    group_ok = jnp.zeros((n_rows, LANES), jnp.bool_)
    for _ in range(TOPK_GROUPS):
        p = pick_best(cur, group)
        group_ok = group_ok | p
        cur = jnp.where(p, NEG_INF, cur)

    cur = jnp.where(group_ok & valid, sel, NEG_INF)
    picked = jnp.zeros((n_rows, LANES), jnp.bool_)
    for _ in range(TOP_K):
        p = pick_best(cur, lane)
        picked = picked | p
        cur = jnp.where(p, NEG_INF, cur)

    w = jnp.where(picked, scores, 0.0)
    return w / jnp.sum(w, axis=1, keepdims=True) * ROUTED_SCALE


def _router_kernel(x_ref, nw_ref, shl, scl, shc, scc, rw_ref, rb_ref, h_ref, g_ref, *, lay):
    _modnorm_rows(lay, x_ref, nw_ref, shl, scl, shc, scc, h_ref)
    slab = _pick(lay.tm, (64, 32, 16))
    rb = rb_ref[...]
    lane_i = lax.broadcasted_iota(jnp.int32, (slab, LANES), 1)

    def body(s, carry):
        rows = pl.ds(pl.multiple_of(s * slab, slab), slab)
        logits = jnp.dot(h_ref[rows, :], rw_ref[...], preferred_element_type=F32)
        gates = _route(logits, rb)
        g_ref[rows, :] = jnp.where(lane_i == N_EXPERTS, 1.0, gates)
        return carry

    lax.fori_loop(0, lay.tm // slab, body, 0)


def _router(lay, x, nw, mods, rw, rb):
    d = lay.d
    shl, shc = _mod_specs(lay, 3, two_axes=False)
    scl, scc = _mod_specs(lay, 4, two_axes=False)
    return pl.pallas_call(
        functools.partial(_router_kernel, lay=lay),
        grid=(lay.n_tiles,),
        in_specs=[
            pl.BlockSpec((lay.tm, d), lambda i: (i, 0)),
            pl.BlockSpec((1, d), lambda i: (0, 0)),
            shl, scl, shc, scc,
            pl.BlockSpec((d, LANES), lambda i: (0, 0)),
            pl.BlockSpec((1, LANES), lambda i: (0, 0)),
        ],
        out_specs=[pl.BlockSpec((lay.tm, d), lambda i: (i, 0)),
                   pl.BlockSpec((lay.tm, LANES), lambda i: (i, 0))],
        out_shape=[jax.ShapeDtypeStruct((lay.t, d), BF16),
                   jax.ShapeDtypeStruct((lay.t, LANES), F32)],
        compiler_params=_params("arbitrary"),
        name="moe_router",
    )(x, nw, mods, mods, mods, mods, rw, rb)


def _expert_in_kernel(h_ref, w_ref, g_ref, o_ref, *, f):
    e = pl.program_id(1)
    hu = jnp.dot(h_ref[...], w_ref[0], preferred_element_type=F32)
    gt, up = hu[:, :f], hu[:, f:]
    lane_i = lax.broadcasted_iota(jnp.int32, g_ref.shape, 1)
    gate = jnp.sum(jnp.where(lane_i == e, g_ref[...], 0.0), axis=1, keepdims=True)
    o_ref[...] = (jax.nn.silu(gt) * up * gate).astype(o_ref.dtype)


def _expert_in(lay, h, w_all, gates):
    ne, d, f2 = w_all.shape
    f = f2 // 2
    return pl.pallas_call(
        functools.partial(_expert_in_kernel, f=f),
        grid=(lay.n_tiles, ne),
        in_specs=[
            pl.BlockSpec((lay.tm, d), lambda i, e: (i, 0)),
            pl.BlockSpec((1, d, f2), lambda i, e: (e, 0, 0)),
            pl.BlockSpec((lay.tm, LANES), lambda i, e: (i, 0)),
        ],
        out_specs=pl.BlockSpec((lay.tm, f), lambda i, e: (i, e)),
        out_shape=jax.ShapeDtypeStruct((lay.t, ne * f), BF16),
        compiler_params=_params("arbitrary", "arbitrary"),
        name="moe_expert_in",
    )(h, w_all, gates)


def _final_norm_kernel(x_ref, w_ref, o_ref):
    x = x_ref[0]
    o_ref[0] = x * lax.rsqrt(jnp.mean(x * x, axis=-1, keepdims=True) + EPS) * w_ref[...]


def _final_norm(lay, x, w):
    d = lay.d
    rows = _pick(lay.seq, (256, 128, 64))
    assert lay.ctx_len % rows == 0
    off = lay.ctx_len // rows
    return pl.pallas_call(
        _final_norm_kernel,
        grid=(lay.batch, lay.seq // rows),
        in_specs=[pl.BlockSpec((1, rows, d), lambda b, j: (b, off + j, 0)),
                  pl.BlockSpec((1, d), lambda b, j: (0, 0))],
        out_specs=pl.BlockSpec((1, rows, d), lambda b, j: (b, j, 0)),
        out_shape=jax.ShapeDtypeStruct((lay.batch, lay.seq, d), F32),
        compiler_params=_params("arbitrary", "arbitrary"),
        name="final_norm",
    )(x.reshape(lay.batch, lay.p, d), w)


def kernel(x, c, ctx, c_ctx, ada_w_down, ada_w_up, ada_b, norm_mix, norm_ffn, conv_w_in, conv_w, conv_w_out, gla_w_in, gla_w_a2_fwd, gla_b_a2_fwd, gla_w_a2_bwd, gla_b_a2_bwd, gla_norm, gla_w_out, router_w, router_bias, exp_w_in, exp_w_out, shared_w_in, shared_w_out, norm_final):
    batch, seq, d = x.shape
    ctx_len = ctx.shape[1]
    depth = ada_w_down.shape[0]
    lay = _Layout(batch, ctx_len, seq, d)
    qk = d // 2
    tn = _pick(d, (512, 256, 128))

    cond = jnp.zeros((MOD_ROWS, d), F32).at[:batch].set(c).at[batch].set(c_ctx)
    mods_all = _ada_all(cond, ada_w_down, ada_w_up, ada_b)
    xs = jnp.concatenate([ctx, x], axis=1).reshape(lay.t, d)

    for i in range(depth):
        j = i // 2
        mods = mods_all[i].reshape(MOD_ROWS, 1, N_MOD * d)
        nw_mix = norm_mix[i].reshape(1, d)
        if i % 2 == 0:
            tc = _pick(d, (256, 128))
            w_in = conv_w_in[j].reshape(d, 3, d // tc, tc).transpose(0, 2, 1, 3).reshape(d, 3 * d)
            y = _norm_proj(lay, xs, nw_mix, mods, 0, 1, w_in.astype(BF16), out_dtype=BF16,
                           tn=3 * tc, conv_w=conv_w[j], name="conv_in")
            xs = _out_proj(lay, y, conv_w_out[j].astype(BF16), xs, mods, 2, tn=tn, name="conv_out")
        else:
            n_main = 2 * qk + 2 * d
            w_main = gla_w_in[j][:, :n_main].astype(BF16)
            w_aux = jnp.zeros((d, LANES), BF16).at[:, :2 * GLA_GATE_RANK].set(
                gla_w_in[j][:, n_main:].astype(BF16))
            qkvr, a = _norm_proj(lay, xs, nw_mix, mods, 0, 1, w_main, out_dtype=BF16, tn=tn,
                                 w_aux=w_aux, name="gla_in")
            waf = jnp.zeros((LANES, qk), BF16).at[:GLA_GATE_RANK].set(gla_w_a2_fwd[j].astype(BF16))
            wab = jnp.zeros((LANES, qk), BF16).at[GLA_GATE_RANK:2 * GLA_GATE_RANK].set(
                gla_w_a2_bwd[j].astype(BF16))
            y = _gla(lay, qkvr, a, waf, gla_b_a2_fwd[j].reshape(1, qk), wab,
                     gla_b_a2_bwd[j].reshape(1, qk), gla_norm[j].reshape(1, d // GLA_HEADS))
            xs = _out_proj(lay, y, gla_w_out[j].astype(BF16), xs, mods, 2, tn=tn, name="gla_out")

        rw = jnp.zeros((d, LANES), BF16).at[:, :N_EXPERTS].set(router_w[i].astype(BF16))
        rb = jnp.zeros((1, LANES), F32).at[0, :N_EXPERTS].set(router_bias[i])
        h, gates = _router(lay, xs, norm_ffn[i].reshape(1, d), mods, rw, rb)
        w_all = jnp.concatenate([exp_w_in[i], shared_w_in[i][None]], axis=0).astype(BF16)
        act = _expert_in(lay, h, w_all, gates)
        w_out_all = jnp.concatenate([exp_w_out[i].reshape(-1, d), shared_w_out[i]], axis=0).astype(BF16)
        xs = _out_proj(lay, act, w_out_all, xs, mods, 5, tn=tn, name="moe_out")

    return _final_norm(lay, xs, norm_final.reshape(1, d))
```

```python
import functools

import jax
import jax.numpy as jnp
import numpy as np
from jax import lax
from jax.experimental import pallas as pl
from jax.experimental.pallas import tpu as pltpu

GRID_W = 64
EPS = 1e-6
N_MOD = 6
GLA_HEADS = 8
GLA_GATE_RANK = 16
GLA_GATE_TAU = 16.0
GLA_LOG_DECAY_FLOOR = -1.0
N_EXPERTS = 64
N_GROUPS = 8
TOPK_GROUPS = 4
TOP_K = 8
ROUTED_SCALE = 2.5

LANES = 128
GLA_ROWS = 128
TILES_PER_SAMPLE = 4
MOD_ROWS = 16
VMEM_LIMIT_BYTES = 56 * 1024 * 1024

MOE_ROWS = 256
COMBINE_ROWS = 128
MOE_EXTRA_BLOCKS = 5
MOE_ID_RING = 8

BF16 = jnp.bfloat16
F32 = jnp.float32
U32 = jnp.uint32
HIGH_HALF = np.uint32(0xFFFF0000)
NEG_INF = float("-inf")


def _params(*sem):
    return pltpu.CompilerParams(dimension_semantics=sem, vmem_limit_bytes=VMEM_LIMIT_BYTES)


def _pick(n, prefs):
    for p in prefs:
        if n % p == 0:
            return p
    return n


def _ada_kernel(cond_ref, wd_ref, wu_ref, b_ref, o_ref, z_ref):
    @pl.when(pl.program_id(1) == 0)
    def _():
        s = jax.nn.silu(cond_ref[...]).astype(BF16)
        z_ref[...] = jnp.dot(s, wd_ref[0].astype(BF16), preferred_element_type=F32)

    o_ref[0] = jnp.dot(z_ref[...].astype(BF16), wu_ref[0].astype(BF16),
                       preferred_element_type=F32) + b_ref[0]


def _ada_all(cond, w_down, w_up, b):
    depth, d, r = w_down.shape
    n = w_up.shape[-1]
    tn = _pick(n, (2048, 1024, 512, 256, 128))
    return pl.pallas_call(
        _ada_kernel,
        grid=(depth, n // tn),
        in_specs=[
            pl.BlockSpec((MOD_ROWS, d), lambda l, j: (0, 0)),
            pl.BlockSpec((1, d, r), lambda l, j: (l, 0, 0)),
            pl.BlockSpec((1, r, tn), lambda l, j: (l, 0, j)),
            pl.BlockSpec((1, 1, tn), lambda l, j: (l, 0, j)),
        ],
        out_specs=pl.BlockSpec((1, MOD_ROWS, tn), lambda l, j: (l, 0, j)),
        out_shape=jax.ShapeDtypeStruct((depth, MOD_ROWS, n), F32),
        scratch_shapes=[pltpu.VMEM((MOD_ROWS, r), F32)],
        compiler_params=_params("arbitrary", "arbitrary"),
        name="ada_mod",
    )(cond, w_down, w_up, b.reshape(depth, 1, n))


class _Layout:
    def __init__(self, batch, ctx_len, seq, d):
        self.batch, self.ctx_len, self.seq, self.d = batch, ctx_len, seq, d
        self.p = ctx_len + seq
        self.t = batch * self.p
        assert self.p % TILES_PER_SAMPLE == 0
        self.tm = self.p // TILES_PER_SAMPLE
        assert GRID_W & (GRID_W - 1) == 0
        assert self.tm % GRID_W == 0 and ctx_len % GRID_W == 0 and self.tm % 16 == 0
        assert ctx_len <= self.tm and batch < MOD_ROWS
        self.n_tiles = self.t // self.tm
        self.slab = _pick(self.tm, (32, 16))
        assert ctx_len % self.slab == 0


def _mod_specs(lay, idx, two_axes=True):
    d = lay.d
    if two_axes:
        lat = pl.BlockSpec((1, 1, d), lambda i, j: (i // TILES_PER_SAMPLE, 0, idx))
        ctx = pl.BlockSpec((1, 1, d), lambda i, j: (lay.batch, 0, idx))
    else:
        lat = pl.BlockSpec((1, 1, d), lambda i: (i // TILES_PER_SAMPLE, 0, idx))
        ctx = pl.BlockSpec((1, 1, d), lambda i: (lay.batch, 0, idx))
    return lat, ctx


def _tile_row0(lay):
    return (pl.program_id(0) % TILES_PER_SAMPLE) * lay.tm


def _row_in_sample(lay, shape):
    return _tile_row0(lay) + lax.broadcasted_iota(jnp.int32, shape, 0)


def _pack_halves(x):
    half = x.shape[1] // 2
    xb = x.astype(BF16).astype(F32)
    lo = lax.bitcast_convert_type(xb[:, :half], U32) >> 16
    hi = lax.bitcast_convert_type(xb[:, half:], U32) & HIGH_HALF
    return hi | lo


def _unpack_halves(w):
    lo = lax.bitcast_convert_type(w << 16, F32)
    hi = lax.bitcast_convert_type(w & HIGH_HALF, F32)
    return lo, hi


def _modnorm_rows(lay, x_ref, nw_ref, shl_ref, scl_ref, shc_ref, scc_ref, h_ref, packed_ref=None):
    nw = nw_ref[...]
    slab = lay.slab
    row0 = _tile_row0(lay)

    def body(s, carry):
        r = pl.multiple_of(s * slab, slab)
        x = x_ref[pl.ds(r, slab), :]
        ms = jnp.mean(x * x, axis=-1, keepdims=True)
        y = x * lax.rsqrt(ms + EPS) * nw
        is_ctx = row0 + r < lay.ctx_len
        sc = jnp.where(is_ctx, scc_ref[0], scl_ref[0])
        sh = jnp.where(is_ctx, shc_ref[0], shl_ref[0])
        h = y * (1.0 + sc) + sh
        h_ref[pl.ds(r, slab), :] = h.astype(h_ref.dtype)
        if packed_ref is not None:
            packed_ref[pl.ds(r, slab), :] = _pack_halves(h)
        return carry

    lax.fori_loop(0, lay.tm // slab, body, 0)


def _proj_kernel(x_ref, nw_ref, shl, scl, shc, scc, w_ref, o_ref, h_ref, *, lay):
    @pl.when(pl.program_id(1) == 0)
    def _():
        _modnorm_rows(lay, x_ref, nw_ref, shl, scl, shc, scc, h_ref)

    o_ref[...] = jnp.dot(h_ref[...], w_ref[...], preferred_element_type=F32).astype(o_ref.dtype)


def _proj_aux_kernel(x_ref, nw_ref, shl, scl, shc, scc, w_ref, wa_ref, o_ref, a_ref, h_ref, *, lay):
    @pl.when(pl.program_id(1) == 0)
    def _():
        _modnorm_rows(lay, x_ref, nw_ref, shl, scl, shc, scc, h_ref)
        a_ref[...] = jnp.dot(h_ref[...], wa_ref[...], preferred_element_type=F32)

    o_ref[...] = jnp.dot(h_ref[...], w_ref[...], preferred_element_type=F32).astype(o_ref.dtype)


def _conv_proj_kernel(x_ref, nw_ref, shl, scl, shc, scc, w_ref, cw_ref, o_ref, h_ref, *, lay, tc):
    @pl.when(pl.program_id(1) == 0)
    def _():
        _modnorm_rows(lay, x_ref, nw_ref, shl, scl, shc, scc, h_ref)

    tm = lay.tm
    r = jnp.dot(h_ref[...], w_ref[...], preferred_element_type=F32)
    bg, cg, v = r[:, :tc], r[:, tc:2 * tc], r[:, 2 * tc:]
    u = cg * v
    row = _row_in_sample(lay, (tm, tc))
    is_ctx = row < lay.ctx_len
    col = (row - lay.ctx_len) & (GRID_W - 1)
    first = (is_ctx & (row == 0)) | (~is_ctx & (col == 0))
    last = (is_ctx & (row == lay.ctx_len - 1)) | (~is_ctx & (col == GRID_W - 1))
    prev = jnp.where(first, 0.0, pltpu.roll(u, 1, axis=0))
    nxt = jnp.where(last, 0.0, pltpu.roll(u, tm - 1, axis=0))
    cw = cw_ref[...]
    conv = prev * cw[0:1, :] + u * cw[1:2, :] + nxt * cw[2:3, :]
    o_ref[...] = (bg * conv).astype(o_ref.dtype)


def _norm_proj(lay, x, nw, mods, shift_idx, scale_idx, w, *, out_dtype, tn, w_aux=None,
               conv_w=None, name):
    d, n = w.shape
    shl, shc = _mod_specs(lay, shift_idx)
    scl, scc = _mod_specs(lay, scale_idx)
    in_specs = [
        pl.BlockSpec((lay.tm, d), lambda i, j: (i, 0)),
        pl.BlockSpec((1, d), lambda i, j: (0, 0)),
        shl, scl, shc, scc,
        pl.BlockSpec((d, tn), lambda i, j: (0, j)),
    ]
    args = [x, nw, mods, mods, mods, mods, w]
    scratch = [pltpu.VMEM((lay.tm, d), BF16)]
    params = _params("arbitrary", "arbitrary")
    if conv_w is not None:
        tc = tn // 3
        in_specs.append(pl.BlockSpec((3, tc), lambda i, j: (0, j)))
        return pl.pallas_call(
            functools.partial(_conv_proj_kernel, lay=lay, tc=tc),
            grid=(lay.n_tiles, n // tn), in_specs=in_specs,
            out_specs=pl.BlockSpec((lay.tm, tc), lambda i, j: (i, j)),
            out_shape=jax.ShapeDtypeStruct((lay.t, n // 3), out_dtype),
            scratch_shapes=scratch, compiler_params=params, name=name,
        )(*args, conv_w)
    if w_aux is not None:
        na = w_aux.shape[1]
        in_specs.append(pl.BlockSpec((d, na), lambda i, j: (0, 0)))
        return pl.pallas_call(
            functools.partial(_proj_aux_kernel, lay=lay),
            grid=(lay.n_tiles, n // tn), in_specs=in_specs,
            out_specs=[pl.BlockSpec((lay.tm, tn), lambda i, j: (i, j)),
                       pl.BlockSpec((lay.tm, na), lambda i, j: (i, 0))],
            out_shape=[jax.ShapeDtypeStruct((lay.t, n), out_dtype),
                       jax.ShapeDtypeStruct((lay.t, na), F32)],
            scratch_shapes=scratch, compiler_params=params, name=name,
        )(*args, w_aux)
    return pl.pallas_call(
        functools.partial(_proj_kernel, lay=lay),
        grid=(lay.n_tiles, n // tn), in_specs=in_specs,
        out_specs=pl.BlockSpec((lay.tm, tn), lambda i, j: (i, j)),
        out_shape=jax.ShapeDtypeStruct((lay.t, n), out_dtype),
        scratch_shapes=scratch, compiler_params=params, name=name,
    )(*args)


def _out_proj_kernel(a_ref, w_ref, x_ref, gl_ref, gc_ref, o_ref, *, lay):
    y = jnp.dot(a_ref[...], w_ref[...], preferred_element_type=F32)
    is_ctx = _row_in_sample(lay, y.shape) < lay.ctx_len
    gate = jnp.where(is_ctx, gc_ref[0], gl_ref[0])
    o_ref[...] = x_ref[...] + gate * y


def _out_proj(lay, a, w, x, mods, gate_idx, *, tn, name):
    k, d = w.shape
    gl = pl.BlockSpec((1, 1, tn), lambda i, j: (i // TILES_PER_SAMPLE, 0, gate_idx * (d // tn) + j))
    gc = pl.BlockSpec((1, 1, tn), lambda i, j: (lay.batch, 0, gate_idx * (d // tn) + j))
    return pl.pallas_call(
        functools.partial(_out_proj_kernel, lay=lay),
        grid=(lay.n_tiles, d // tn),
        in_specs=[
            pl.BlockSpec((lay.tm, k), lambda i, j: (i, 0)),
            pl.BlockSpec((k, tn), lambda i, j: (0, j)),
            pl.BlockSpec((lay.tm, tn), lambda i, j: (i, j)),
            gl, gc,
        ],
        out_specs=pl.BlockSpec((lay.tm, tn), lambda i, j: (i, j)),
        out_shape=jax.ShapeDtypeStruct((lay.t, d), F32),
        input_output_aliases={2: 0},
        compiler_params=_params("arbitrary", "arbitrary"),
        name=name,
    )(a, w, x, mods, mods)


def _gla_kernel(q_ref, k_ref, v_ref, r_ref, a_ref, waf_ref, baf_ref, wab_ref, bab_ref, nw_ref,
                y_ref, of_ref, s_ref, *, n_chunks, n_ctx_chunks, dk):
    c_rows = GLA_ROWS
    scale = dk ** -0.5
    ri = lax.broadcasted_iota(jnp.int32, (c_rows, c_rows), 0)
    ci = lax.broadcasted_iota(jnp.int32, (c_rows, c_rows), 1)
    nt = (((1,), (1,)), ((), ()))
    tn_dims = (((0,), (0,)), ((), ()))

    def chunk(c, forward):
        rows = pl.ds(pl.multiple_of(c * c_rows, c_rows), c_rows)
        keep = (ci <= ri) if forward else (ci >= ri)
        tri = keep.astype(BF16)
        wa, ba = (waf_ref, baf_ref) if forward else (wab_ref, bab_ref)
        z = jnp.dot(a_ref[rows, :].astype(BF16), wa[...], preferred_element_type=F32) + ba[...]
        g = jnp.maximum(jax.nn.log_sigmoid(z) / GLA_GATE_TAU, GLA_LOG_DECAY_FLOOR)
        g1 = g.astype(BF16)
        rem = g - g1.astype(F32)
        g2 = rem.astype(BF16)
        g3 = (rem - g2.astype(F32)).astype(BF16)
        bcum = (jnp.dot(tri, g1, preferred_element_type=F32)
                + jnp.dot(tri, g2, preferred_element_type=F32)
                + jnp.dot(tri, g3, preferred_element_type=F32))
        mid = bcum[c_rows // 2:c_rows // 2 + 1, :]
        tot = bcum[c_rows - 1:c_rows, :] if forward else bcum[0:1, :]
        q = q_ref[rows, :].astype(F32) * scale
        k = k_ref[rows, :].astype(F32)
        v = v_ref[rows, :]
        q_rel = (q * jnp.exp(bcum - mid)).astype(BF16)
        k_rel = (k * jnp.exp(mid - bcum)).astype(BF16)
        q_dec = (q * jnp.exp(bcum)).astype(BF16)
        k_end = (k * jnp.exp(tot - bcum)).astype(BF16)
        att = lax.dot_general(q_rel, k_rel, nt, preferred_element_type=F32)
        att = jnp.where(keep, att, 0.0).astype(BF16)
        s = s_ref[...]
        o = (jnp.dot(att, v, preferred_element_type=F32)
             + lax.dot_general(q_dec, s.astype(BF16), nt, preferred_element_type=F32))
        s_ref[...] = jnp.exp(tot) * s + lax.dot_general(v, k_end, tn_dims, preferred_element_type=F32)
        return rows, o

    s_ref[...] = jnp.zeros_like(s_ref)

    def fwd_body(n, carry):
        rows, o = chunk(n, True)
        of_ref[rows, :] = o
        return carry

    lax.fori_loop(0, n_chunks, fwd_body, 0)

    s_ref[...] = jnp.zeros_like(s_ref)
    nw = nw_ref[...]

    def bwd_body(n, carry):
        c = jnp.where(n < n_ctx_chunks, n_ctx_chunks - 1 - n, n_chunks - 1 - (n - n_ctx_chunks))
        rows, o_b = chunk(c, False)
        o = of_ref[rows, :] + o_b
        o = o * lax.rsqrt(jnp.mean(o * o, axis=-1, keepdims=True) + EPS) * nw
        y_ref[rows, :] = (o * jax.nn.silu(r_ref[rows, :].astype(F32))).astype(y_ref.dtype)
        return carry

    lax.fori_loop(0, n_chunks, bwd_body, 0)


def _gla(lay, qkvr, a, waf, baf, wab, bab, norm_w):
    d = lay.d
    dk = d // 2 // GLA_HEADS
    dv = d // GLA_HEADS
    assert lay.ctx_len % GLA_ROWS == 0 and lay.seq % GLA_ROWS == 0
    n_chunks = lay.p // GLA_ROWS
    kern = functools.partial(_gla_kernel, n_chunks=n_chunks,
                             n_ctx_chunks=lay.ctx_len // GLA_ROWS, dk=dk)
    h = GLA_HEADS
    return pl.pallas_call(
        kern,
        grid=(lay.batch, h),
        in_specs=[
            pl.BlockSpec((lay.p, dk), lambda b, i: (b, i)),
            pl.BlockSpec((lay.p, dk), lambda b, i: (b, h + i)),
            pl.BlockSpec((lay.p, dv), lambda b, i: (b, h + i)),
            pl.BlockSpec((lay.p, dv), lambda b, i: (b, 2 * h + i)),
            pl.BlockSpec((lay.p, LANES), lambda b, i: (b, 0)),
            pl.BlockSpec((LANES, dk), lambda b, i: (0, i)),
            pl.BlockSpec((1, dk), lambda b, i: (0, i)),
            pl.BlockSpec((LANES, dk), lambda b, i: (0, i)),
            pl.BlockSpec((1, dk), lambda b, i: (0, i)),
            pl.BlockSpec((1, dv), lambda b, i: (0, 0)),
        ],
        out_specs=pl.BlockSpec((lay.p, dv), lambda b, i: (b, i)),
        out_shape=jax.ShapeDtypeStruct((lay.t, d), BF16),
        scratch_shapes=[pltpu.VMEM((lay.p, dv), F32), pltpu.VMEM((dv, dk), F32)],
        compiler_params=_params("arbitrary", "arbitrary"),
        name="gla_scan",
    )(qkvr, qkvr, qkvr, qkvr, a, waf, baf, wab, bab, norm_w)


def _route(logits, bias):
    n_rows = logits.shape[0]
    per_group = N_EXPERTS // N_GROUPS
    lane_i = lax.broadcasted_iota(jnp.int32, (n_rows, LANES), 1)
    lane = lane_i.astype(F32)
    group = (lane_i // per_group).astype(F32)
    valid = lane_i < N_EXPERTS
    big = float(2 * LANES)
    scores = jax.nn.sigmoid(logits)
    sel = jnp.where(valid, scores + bias, NEG_INF)

    def partner(x, s):
        return jnp.where((lane_i & s) == 0, pltpu.roll(x, LANES - s, axis=1), pltpu.roll(x, s, axis=1))

    def group_reduce(x, op):
        s = 1
        while s < per_group:
            x = op(x, partner(x, s))
            s *= 2
        return x

    m1 = group_reduce(sel, jnp.maximum)
    first = group_reduce(jnp.where(sel == m1, lane, big), jnp.minimum)
    m2 = group_reduce(jnp.where(lane == first, NEG_INF, sel), jnp.maximum)
    cur = jnp.where(valid, m1 + m2, NEG_INF)

    def pick_best(cur, ident):
        m = jnp.max(cur, axis=1, keepdims=True)
        best = jnp.min(jnp.where(cur == m, ident, big), axis=1, keepdims=True)
        return ident == best

    group_ok = jnp.zeros((n_rows, LANES), jnp.bool_)
    for _ in range(TOPK_GROUPS):
        p = pick_best(cur, group)
        group_ok = group_ok | p
        cur = jnp.where(p, NEG_INF, cur)

    cur = jnp.where(group_ok & valid, sel, NEG_INF)
    picked = jnp.zeros((n_rows, LANES), jnp.bool_)
    ids = jnp.zeros((n_rows, LANES), F32)
    for it in range(TOP_K):
        m = jnp.max(cur, axis=1, keepdims=True)
        best = jnp.min(jnp.where(cur == m, lane, big), axis=1, keepdims=True)
        p = lane == best
        picked = picked | p
        cur = jnp.where(p, NEG_INF, cur)
        ids = jnp.where(lane_i == it, best, ids)

    w = jnp.where(picked, scores, 0.0)
    gates = w / jnp.sum(w, axis=1, keepdims=True) * ROUTED_SCALE
    return gates, ids.astype(jnp.int32)


def _router_kernel(x_ref, nw_ref, shl, scl, shc, scc, rw_ref, rb_ref, hp_ref, g_ref, id_ref, h_ref, *, lay):
    _modnorm_rows(lay, x_ref, nw_ref, shl, scl, shc, scc, h_ref, packed_ref=hp_ref)
    logits = jnp.dot(h_ref[...], rw_ref[...], preferred_element_type=F32)
    g_ref[...], id_ref[...] = _route(logits, rb_ref[...])


def _router(lay, x, nw, mods, rw, rb):
    d = lay.d
    shl, shc = _mod_specs(lay, 3, two_axes=False)
    scl, scc = _mod_specs(lay, 4, two_axes=False)
    return pl.pallas_call(
        functools.partial(_router_kernel, lay=lay),
        grid=(lay.n_tiles,),
        in_specs=[
            pl.BlockSpec((lay.tm, d), lambda i: (i, 0)),
            pl.BlockSpec((1, d), lambda i: (0, 0)),
            shl, scl, shc, scc,
            pl.BlockSpec((d, LANES), lambda i: (0, 0)),
            pl.BlockSpec((1, LANES), lambda i: (0, 0)),
        ],
        out_specs=[pl.BlockSpec((lay.tm, d // 2), lambda i: (i, 0)),
                   pl.BlockSpec((lay.tm, LANES), lambda i: (i, 0)),
                   pl.BlockSpec((lay.tm, LANES), lambda i: (i, 0))],
        out_shape=[jax.ShapeDtypeStruct((lay.t, d // 2), U32),
                   jax.ShapeDtypeStruct((lay.t, LANES), F32),
                   jax.ShapeDtypeStruct((lay.t, LANES), jnp.int32)],
        scratch_shapes=[pltpu.VMEM((lay.tm, d), BF16)],
        compiler_params=_params("arbitrary"),
        name="moe_router",
    )(x, nw, mods, mods, mods, mods, rw, rb)


def _moe_plan(lay, gates, ids):
    t, rb = lay.t, MOE_ROWS
    n_pairs = t * TOP_K
    n_blocks = n_pairs // rb + N_EXPERTS
    eidx = ids[:, :TOP_K]
    w = jnp.take_along_axis(gates, eidx, axis=1).reshape(-1)
    flat_e = eidx.reshape(-1)
    order = jnp.argsort(flat_e, stable=True).astype(jnp.int32)
    e_sorted = flat_e[order]
    counts = jnp.sum(flat_e[:, None] == jnp.arange(N_EXPERTS, dtype=jnp.int32)[None, :], axis=0,
                     dtype=jnp.int32)
    padded = (counts + rb - 1) // rb * rb
    pad_end = jnp.cumsum(padded)
    start = jnp.cumsum(counts) - counts
    dest = (pad_end - padded)[e_sorted] + jnp.arange(n_pairs, dtype=jnp.int32) - start[e_sorted]
    n_table = (n_blocks + MOE_EXTRA_BLOCKS) * rb
    row = jnp.arange(n_table, dtype=jnp.int32)
    in_block = row % rb
    pad_dst = TOP_K * t + ((row // rb + 1) % 2) * rb + in_block
    tok = in_block.at[dest + rb].set(order // TOP_K)
    dst = pad_dst.at[dest + rb].set((order % TOP_K) * t + order // TOP_K)
    gate = jnp.zeros((n_blocks * rb,), F32).at[dest].set(w[order])
    n_used = pad_end[-1] // rb
    block = jnp.minimum(jnp.arange(n_blocks + 1, dtype=jnp.int32), n_used - 1)
    block_expert = jnp.searchsorted(pad_end, block * rb, side="right").astype(jnp.int32)
    shape = (n_blocks + MOE_EXTRA_BLOCKS, rb)
    return tok.reshape(shape), dst.reshape(shape), gate.reshape(n_blocks * rb, 1), block_expert


def _moe_expert_kernel(be_ref, tok_hbm, dst_hbm, g_ref, win_ref, wout_ref, h_hbm, y_hbm,
                       hbuf, ybuf, winb, woutb, tok_smem, dst_smem, gsem, ssem, isem,
                       *, t_rows, n_blocks, f):
    rb = MOE_ROWS
    nb = pl.program_id(0)
    ys = nb % 2
    hs = nb % 3
    half = hbuf.shape[-1]
    dump0 = TOP_K * t_rows

    def ids_copies(block):
        k = (block + 1) % MOE_ID_RING
        return (pltpu.make_async_copy(tok_hbm.at[block + 1], tok_smem.at[k], isem.at[0]),
                pltpu.make_async_copy(dst_hbm.at[block + 1], dst_smem.at[k], isem.at[1]))

    def gather_row(k, r, s):
        return pltpu.make_async_copy(h_hbm.at[pl.ds(tok_smem[k, r], 1)], hbuf.at[s, pl.ds(r, 1)],
                                     gsem.at[s])

    def scatter_row(k, r, s):
        return pltpu.make_async_copy(ybuf.at[s, pl.ds(r, 1)], y_hbm.at[pl.ds(dst_smem[k, r], 1)],
                                     ssem.at[s])

    def gather_wait(s):
        pltpu.make_async_copy(h_hbm.at[pl.ds(0, rb)], hbuf.at[s], gsem.at[s]).wait()

    def scatter_wait(s):
        pltpu.make_async_copy(ybuf.at[s], y_hbm.at[pl.ds(0, rb)], ssem.at[s]).wait()

    @pl.when(nb == 0)
    def _():
        for block in (-1, 0, 1, 2):
            for c in ids_copies(block):
                c.start()
                c.wait()
        ybuf[...] = jnp.zeros_like(ybuf)
        pltpu.make_async_copy(ybuf.at[0], y_hbm.at[pl.ds(dump0, rb)], ssem.at[0]).start()
        for block in (0, 1):
            def body(r, carry, block=block):
                gather_row(block + 1, r, block).start()
                return carry
            lax.fori_loop(0, rb, body, 0)

    @pl.when(nb > 0)
    def _():
        for c in ids_copies(nb + 2):
            c.wait()

    for c in ids_copies(nb + 3):
        c.start()

    @pl.when((nb == 0) | (be_ref[nb] != be_ref[jnp.maximum(nb - 1, 0)]))
    def _():
        winb[...] = win_ref[0].astype(BF16)
        woutb[...] = wout_ref[0].astype(BF16)

    gk = (nb + 3) % MOE_ID_RING
    sk = nb % MOE_ID_RING
    ghs = (nb + 2) % 3
    n_groups = 8
    per_group = rb // n_groups

    def issue(group):
        rows = range(group * per_group, (group + 1) * per_group)
        copies = [gather_row(gk, r, ghs) for r in rows] + [scatter_row(sk, r, 1 - ys) for r in rows]
        for i, c in enumerate(copies):
            c.start(priority=i % 2)

    gather_wait(hs)
    issue(0)
    lo, hi = _unpack_halves(hbuf[hs])
    lo, hi = lo.astype(BF16), hi.astype(BF16)
    issue(1)
    hu = jnp.dot(lo, winb[:half, :], preferred_element_type=F32)
    issue(2)
    hu = hu + jnp.dot(hi, winb[half:, :], preferred_element_type=F32)
    issue(3)
    act = (jax.nn.silu(hu[:, :f]) * hu[:, f:] * g_ref[...]).astype(BF16)
    y = jnp.dot(act, woutb[...], preferred_element_type=F32)
    issue(4)
    scatter_wait(ys)
    cols = half // 4
    for piece in range(4):
        c0 = piece * cols
        yb_lo = y[:, c0:c0 + cols].astype(BF16).astype(F32)
        yb_hi = y[:, half + c0:half + c0 + cols].astype(BF16).astype(F32)
        ybuf[ys, :, c0:c0 + cols] = ((lax.bitcast_convert_type(yb_hi, U32) & HIGH_HALF)
                                     | (lax.bitcast_convert_type(yb_lo, U32) >> 16))
        if piece < 3:
            issue(5 + piece)

    @pl.when(nb == n_blocks)
    def _():
        scatter_wait(1 - ys)
        gather_wait((nb + 1) % 3)
        gather_wait((nb + 2) % 3)
        for c in ids_copies(nb + 3):
            c.wait()


def _moe_experts(lay, hp, tok, dst, gate, block_expert, w_in, w_out):
    ne, d, f2 = w_in.shape
    f = f2 // 2
    rb = MOE_ROWS
    half = d // 2
    n_blocks = gate.shape[0] // rb
    assert tok.shape == dst.shape == (n_blocks + MOE_EXTRA_BLOCKS, rb) and rb <= lay.t
    kern = functools.partial(_moe_expert_kernel, t_rows=lay.t, n_blocks=n_blocks, f=f)
    last = n_blocks - 1
    grid_spec = pltpu.PrefetchScalarGridSpec(
        num_scalar_prefetch=1,
        grid=(n_blocks + 1,),
        in_specs=[
            pl.BlockSpec(memory_space=pl.ANY),
            pl.BlockSpec(memory_space=pl.ANY),
            pl.BlockSpec((rb, 1), lambda nb, be: (jnp.minimum(nb, last), 0)),
            pl.BlockSpec((1, d, f2), lambda nb, be: (be[nb], 0, 0)),
            pl.BlockSpec((1, f, d), lambda nb, be: (be[nb], 0, 0)),
            pl.BlockSpec(memory_space=pl.ANY),
        ],
        out_specs=pl.BlockSpec(memory_space=pl.ANY),
        scratch_shapes=[
            pltpu.VMEM((3, rb, half), U32),
            pltpu.VMEM((2, rb, half), U32),
            pltpu.VMEM((d, f2), BF16),
            pltpu.VMEM((f, d), BF16),
            pltpu.SMEM((MOE_ID_RING, rb), jnp.int32),
            pltpu.SMEM((MOE_ID_RING, rb), jnp.int32),
            pltpu.SemaphoreType.DMA((3,)),
            pltpu.SemaphoreType.DMA((2,)),
            pltpu.SemaphoreType.DMA((2,)),
        ],
    )
    return pl.pallas_call(
        kern,
        grid_spec=grid_spec,
        out_shape=jax.ShapeDtypeStruct((TOP_K * lay.t + 2 * rb, half), U32),
        compiler_params=_params("arbitrary"),
        name="moe_experts",
    )(block_expert, tok, dst, gate, w_in, w_out, hp)


def _moe_combine_kernel(*refs, lay, f):
    y_refs = refs[:TOP_K]
    h_ref, wsi_ref, wso_ref, x_ref, gl_ref, gc_ref, o_ref = refs[TOP_K:]
    half = h_ref.shape[-1]
    lo, hi = _unpack_halves(y_refs[0][...])
    for y_ref in y_refs[1:]:
        l, h = _unpack_halves(y_ref[...])
        lo, hi = lo + l, hi + h
    hlo, hhi = _unpack_halves(h_ref[...])
    hu = (jnp.dot(hlo.astype(BF16), wsi_ref[:half, :], preferred_element_type=F32)
          + jnp.dot(hhi.astype(BF16), wsi_ref[half:, :], preferred_element_type=F32))
    act = (jax.nn.silu(hu[:, :f]) * hu[:, f:]).astype(BF16)
    shared = jnp.dot(act, wso_ref[...], preferred_element_type=F32)
    is_ctx = (pl.program_id(0) * COMBINE_ROWS) % lay.p < lay.ctx_len
    gate = jnp.where(is_ctx, gc_ref[0], gl_ref[0])
    o_ref[:, :half] = x_ref[:, :half] + gate[:, :half] * (lo + shared[:, :half])
    o_ref[:, half:] = x_ref[:, half:] + gate[:, half:] * (hi + shared[:, half:])


def _moe_combine(lay, y, hp, ws_in, ws_out, x, mods, gate_idx):
    d = lay.d
    half = d // 2
    rt = COMBINE_ROWS
    assert lay.ctx_len % rt == 0 and lay.p % rt == 0
    steps = lay.t // rt
    f = ws_out.shape[0]
    y_specs = [pl.BlockSpec((rt, half), functools.partial(lambda i, k: (k * steps + i, 0), k=k))
               for k in range(TOP_K)]
    return pl.pallas_call(
        functools.partial(_moe_combine_kernel, lay=lay, f=f),
        grid=(steps,),
        in_specs=y_specs + [
            pl.BlockSpec((rt, half), lambda i: (i, 0)),
            pl.BlockSpec((d, 2 * f), lambda i: (0, 0)),
            pl.BlockSpec((f, d), lambda i: (0, 0)),
            pl.BlockSpec((rt, d), lambda i: (i, 0)),
            pl.BlockSpec((1, 1, d), lambda i: (i * rt // lay.p, 0, gate_idx)),
            pl.BlockSpec((1, 1, d), lambda i: (lay.batch, 0, gate_idx)),
        ],
        out_specs=pl.BlockSpec((rt, d), lambda i: (i, 0)),
        out_shape=jax.ShapeDtypeStruct((lay.t, d), F32),
        input_output_aliases={TOP_K + 3: 0},
        compiler_params=_params("arbitrary"),
        name="moe_combine",
    )(*([y] * TOP_K), hp, ws_in, ws_out, x, mods, mods)


def _final_norm_kernel(x_ref, w_ref, o_ref):
    x = x_ref[0]
    o_ref[0] = x * lax.rsqrt(jnp.mean(x * x, axis=-1, keepdims=True) + EPS) * w_ref[...]


def _final_norm(lay, x, w):
    d = lay.d
    rows = _pick(lay.seq, (256, 128, 64))
    assert lay.ctx_len % rows == 0
    off = lay.ctx_len // rows
    return pl.pallas_call(
        _final_norm_kernel,
        grid=(lay.batch, lay.seq // rows),
        in_specs=[pl.BlockSpec((1, rows, d), lambda b, j: (b, off + j, 0)),
                  pl.BlockSpec((1, d), lambda b, j: (0, 0))],
        out_specs=pl.BlockSpec((1, rows, d), lambda b, j: (b, j, 0)),
        out_shape=jax.ShapeDtypeStruct((lay.batch, lay.seq, d), F32),
        compiler_params=_params("arbitrary", "arbitrary"),
        name="final_norm",
    )(x.reshape(lay.batch, lay.p, d), w)


def kernel(x, c, ctx, c_ctx, ada_w_down, ada_w_up, ada_b, norm_mix, norm_ffn, conv_w_in, conv_w, conv_w_out, gla_w_in, gla_w_a2_fwd, gla_b_a2_fwd, gla_w_a2_bwd, gla_b_a2_bwd, gla_norm, gla_w_out, router_w, router_bias, exp_w_in, exp_w_out, shared_w_in, shared_w_out, norm_final):
    batch, seq, d = x.shape
    ctx_len = ctx.shape[1]
    depth = ada_w_down.shape[0]
    lay = _Layout(batch, ctx_len, seq, d)
    qk = d // 2
    tn = _pick(d, (512, 256, 128))

    cond = jnp.zeros((MOD_ROWS, d), F32).at[:batch].set(c).at[batch].set(c_ctx)
    mods_all = _ada_all(cond, ada_w_down, ada_w_up, ada_b)
    xs = jnp.concatenate([ctx, x], axis=1).reshape(lay.t, d)

    for i in range(depth):
        j = i // 2
        mods = mods_all[i].reshape(MOD_ROWS, 1, N_MOD * d)
        nw_mix = norm_mix[i].reshape(1, d)
        if i % 2 == 0:
            tc = _pick(d, (256, 128))
            w_in = conv_w_in[j].reshape(d, 3, d // tc, tc).transpose(0, 2, 1, 3).reshape(d, 3 * d)
            y = _norm_proj(lay, xs, nw_mix, mods, 0, 1, w_in.astype(BF16), out_dtype=BF16,
                           tn=3 * tc, conv_w=conv_w[j], name="conv_in")
            xs = _out_proj(lay, y, conv_w_out[j].astype(BF16), xs, mods, 2, tn=tn, name="conv_out")
        else:
            n_main = 2 * qk + 2 * d
            w_main = gla_w_in[j][:, :n_main].astype(BF16)
            w_aux = jnp.zeros((d, LANES), BF16).at[:, :2 * GLA_GATE_RANK].set(
                gla_w_in[j][:, n_main:].astype(BF16))
            qkvr, a = _norm_proj(lay, xs, nw_mix, mods, 0, 1, w_main, out_dtype=BF16, tn=tn,
                                 w_aux=w_aux, name="gla_in")
            waf = jnp.zeros((LANES, qk), BF16).at[:GLA_GATE_RANK].set(gla_w_a2_fwd[j].astype(BF16))
            wab = jnp.zeros((LANES, qk), BF16).at[GLA_GATE_RANK:2 * GLA_GATE_RANK].set(
                gla_w_a2_bwd[j].astype(BF16))
            y = _gla(lay, qkvr, a, waf, gla_b_a2_fwd[j].reshape(1, qk), wab,
                     gla_b_a2_bwd[j].reshape(1, qk), gla_norm[j].reshape(1, d // GLA_HEADS))
            xs = _out_proj(lay, y, gla_w_out[j].astype(BF16), xs, mods, 2, tn=tn, name="gla_out")

        rw = jnp.zeros((d, LANES), BF16).at[:, :N_EXPERTS].set(router_w[i].astype(BF16))
        rb = jnp.zeros((1, LANES), F32).at[0, :N_EXPERTS].set(router_bias[i])
        hp, gates, ids = _router(lay, xs, norm_ffn[i].reshape(1, d), mods, rw, rb)
        tok, dst, gate, block_expert = _moe_plan(lay, gates, ids)
        y = _moe_experts(lay, hp, tok, dst, gate, block_expert, exp_w_in[i], exp_w_out[i])
        xs = _moe_combine(lay, y, hp, shared_w_in[i].astype(BF16), shared_w_out[i].astype(BF16),
                          xs, mods, 5)

    return _final_norm(lay, xs, norm_final.reshape(1, d))
```

```python
import functools

import jax
import jax.numpy as jnp
import numpy as np
from jax import lax
from jax.experimental import pallas as pl
from jax.experimental.pallas import tpu as pltpu

GRID_W = 64
EPS = 1e-6
N_MOD = 6
GLA_HEADS = 8
GLA_GATE_RANK = 16
GLA_GATE_TAU = 16.0
GLA_LOG_DECAY_FLOOR = -1.0
N_EXPERTS = 64
N_GROUPS = 8
TOPK_GROUPS = 4
TOP_K = 8
ROUTED_SCALE = 2.5

LANES = 128
GLA_ROWS = 128
TILES_PER_SAMPLE = 4
MOD_ROWS = 16
VMEM_LIMIT_BYTES = 56 * 1024 * 1024

MOE_ROWS = 256
COMBINE_ROWS = 128
MOE_EXTRA_BLOCKS = 5
MOE_ID_RING = 8

BF16 = jnp.bfloat16
F32 = jnp.float32
U32 = jnp.uint32
HIGH_HALF = np.uint32(0xFFFF0000)
NEG_INF = float("-inf")


def _params(*sem):
    return pltpu.CompilerParams(dimension_semantics=sem, vmem_limit_bytes=VMEM_LIMIT_BYTES)


def _pick(n, prefs):
    for p in prefs:
        if n % p == 0:
            return p
    return n


def _ada_kernel(cond_ref, wd_ref, wu_ref, b_ref, o_ref, z_ref):
    @pl.when(pl.program_id(1) == 0)
    def _():
        s = jax.nn.silu(cond_ref[...]).astype(BF16)
        z_ref[...] = jnp.dot(s, wd_ref[0].astype(BF16), preferred_element_type=F32)

    o_ref[0] = jnp.dot(z_ref[...].astype(BF16), wu_ref[0].astype(BF16),
                       preferred_element_type=F32) + b_ref[0]


def _ada_all(cond, w_down, w_up, b):
    depth, d, r = w_down.shape
    n = w_up.shape[-1]
    tn = _pick(n, (2048, 1024, 512, 256, 128))
    return pl.pallas_call(
        _ada_kernel,
        grid=(depth, n // tn),
        in_specs=[
            pl.BlockSpec((MOD_ROWS, d), lambda l, j: (0, 0)),
            pl.BlockSpec((1, d, r), lambda l, j: (l, 0, 0)),
            pl.BlockSpec((1, r, tn), lambda l, j: (l, 0, j)),
            pl.BlockSpec((1, 1, tn), lambda l, j: (l, 0, j)),
        ],
        out_specs=pl.BlockSpec((1, MOD_ROWS, tn), lambda l, j: (l, 0, j)),
        out_shape=jax.ShapeDtypeStruct((depth, MOD_ROWS, n), F32),
        scratch_shapes=[pltpu.VMEM((MOD_ROWS, r), F32)],
        compiler_params=_params("arbitrary", "arbitrary"),
        name="ada_mod",
    )(cond, w_down, w_up, b.reshape(depth, 1, n))


class _Layout:
    def __init__(self, batch, ctx_len, seq, d):
        self.batch, self.ctx_len, self.seq, self.d = batch, ctx_len, seq, d
        self.p = ctx_len + seq
        self.t = batch * self.p
        assert self.p % TILES_PER_SAMPLE == 0
        self.tm = self.p // TILES_PER_SAMPLE
        assert GRID_W & (GRID_W - 1) == 0
        assert self.tm % GRID_W == 0 and ctx_len % GRID_W == 0 and self.tm % 16 == 0
        assert ctx_len <= self.tm and batch < MOD_ROWS
        self.n_tiles = self.t // self.tm
        self.slab = _pick(self.tm, (32, 16))
        assert ctx_len % self.slab == 0


def _mod_specs(lay, idx, two_axes=True):
    d = lay.d
    if two_axes:
        lat = pl.BlockSpec((1, 1, d), lambda i, j: (i // TILES_PER_SAMPLE, 0, idx))
        ctx = pl.BlockSpec((1, 1, d), lambda i, j: (lay.batch, 0, idx))
    else:
        lat = pl.BlockSpec((1, 1, d), lambda i: (i // TILES_PER_SAMPLE, 0, idx))
        ctx = pl.BlockSpec((1, 1, d), lambda i: (lay.batch, 0, idx))
    return lat, ctx


def _tile_row0(lay):
    return (pl.program_id(0) % TILES_PER_SAMPLE) * lay.tm


def _row_in_sample(lay, shape):
    return _tile_row0(lay) + lax.broadcasted_iota(jnp.int32, shape, 0)


def _pack_halves(x):
    half = x.shape[1] // 2
    xb = x.astype(BF16).astype(F32)
    lo = lax.bitcast_convert_type(xb[:, :half], U32) >> 16
    hi = lax.bitcast_convert_type(xb[:, half:], U32) & HIGH_HALF
    return hi | lo


def _unpack_halves(w):
    lo = lax.bitcast_convert_type(w << 16, F32)
    hi = lax.bitcast_convert_type(w & HIGH_HALF, F32)
    return lo, hi


def _modnorm_rows(lay, x_ref, nw_ref, shl_ref, scl_ref, shc_ref, scc_ref, h_ref, packed_ref=None):
    nw = nw_ref[...]
    slab = lay.slab
    row0 = _tile_row0(lay)

    def body(s, carry):
        r = pl.multiple_of(s * slab, slab)
        x = x_ref[pl.ds(r, slab), :]
        ms = jnp.mean(x * x, axis=-1, keepdims=True)
        y = x * lax.rsqrt(ms + EPS) * nw
        is_ctx = row0 + r < lay.ctx_len
        sc = jnp.where(is_ctx, scc_ref[0], scl_ref[0])
        sh = jnp.where(is_ctx, shc_ref[0], shl_ref[0])
        h = y * (1.0 + sc) + sh
        h_ref[pl.ds(r, slab), :] = h.astype(h_ref.dtype)
        if packed_ref is not None:
            packed_ref[pl.ds(r, slab), :] = _pack_halves(h)
        return carry

    lax.fori_loop(0, lay.tm // slab, body, 0)


def _proj_kernel(x_ref, nw_ref, shl, scl, shc, scc, w_ref, o_ref, h_ref, *, lay):
    @pl.when(pl.program_id(1) == 0)
    def _():
        _modnorm_rows(lay, x_ref, nw_ref, shl, scl, shc, scc, h_ref)

    o_ref[...] = jnp.dot(h_ref[...], w_ref[...], preferred_element_type=F32).astype(o_ref.dtype)


def _proj_aux_kernel(x_ref, nw_ref, shl, scl, shc, scc, w_ref, wa_ref, o_ref, a_ref, h_ref, *, lay):
    @pl.when(pl.program_id(1) == 0)
    def _():
        _modnorm_rows(lay, x_ref, nw_ref, shl, scl, shc, scc, h_ref)
        a_ref[...] = jnp.dot(h_ref[...], wa_ref[...], preferred_element_type=F32)

    o_ref[...] = jnp.dot(h_ref[...], w_ref[...], preferred_element_type=F32).astype(o_ref.dtype)


def _conv_proj_kernel(x_ref, nw_ref, shl, scl, shc, scc, wb_ref, wc_ref, wv_ref, cw_ref, o_ref, h_ref,
                      *, lay, tc):
    @pl.when(pl.program_id(1) == 0)
    def _():
        _modnorm_rows(lay, x_ref, nw_ref, shl, scl, shc, scc, h_ref)

    tm = lay.tm
    h = h_ref[...]
    bg = jnp.dot(h, wb_ref[...], preferred_element_type=F32)
    cg = jnp.dot(h, wc_ref[...], preferred_element_type=F32)
    v = jnp.dot(h, wv_ref[...], preferred_element_type=F32)
    u = cg * v
    row = _row_in_sample(lay, (tm, tc))
    is_ctx = row < lay.ctx_len
    col = (row - lay.ctx_len) & (GRID_W - 1)
    first = (is_ctx & (row == 0)) | (~is_ctx & (col == 0))
    last = (is_ctx & (row == lay.ctx_len - 1)) | (~is_ctx & (col == GRID_W - 1))
    prev = jnp.where(first, 0.0, pltpu.roll(u, 1, axis=0))
    nxt = jnp.where(last, 0.0, pltpu.roll(u, tm - 1, axis=0))
    cw = cw_ref[...]
    conv = prev * cw[0:1, :] + u * cw[1:2, :] + nxt * cw[2:3, :]
    o_ref[...] = (bg * conv).astype(o_ref.dtype)


def _norm_proj(lay, x, nw, mods, shift_idx, scale_idx, w, *, out_dtype, tn, w_aux=None,
               conv_w=None, name):
    d, n = w.shape
    shl, shc = _mod_specs(lay, shift_idx)
    scl, scc = _mod_specs(lay, scale_idx)
    in_specs = [
        pl.BlockSpec((lay.tm, d), lambda i, j: (i, 0)),
        pl.BlockSpec((1, d), lambda i, j: (0, 0)),
        shl, scl, shc, scc,
        pl.BlockSpec((d, tn), lambda i, j: (0, j)),
    ]
    args = [x, nw, mods, mods, mods, mods, w]
    scratch = [pltpu.VMEM((lay.tm, d), BF16)]
    params = _params("arbitrary", "arbitrary")
    if conv_w is not None:
        tc = tn
        nct = n // 3 // tc
        in_specs[-1:] = [pl.BlockSpec((d, tc), functools.partial(lambda i, j, k: (0, k * nct + j), k=k))
                         for k in range(3)]
        in_specs.append(pl.BlockSpec((3, tc), lambda i, j: (0, j)))
        return pl.pallas_call(
            functools.partial(_conv_proj_kernel, lay=lay, tc=tc),
            grid=(lay.n_tiles, nct), in_specs=in_specs,
            out_specs=pl.BlockSpec((lay.tm, tc), lambda i, j: (i, j)),
            out_shape=jax.ShapeDtypeStruct((lay.t, n // 3), out_dtype),
            scratch_shapes=scratch, compiler_params=params, name=name,
        )(*args[:-1], w, w, w, conv_w)
    if w_aux is not None:
        na = w_aux.shape[1]
        in_specs.append(pl.BlockSpec((d, na), lambda i, j: (0, 0)))
        return pl.pallas_call(
            functools.partial(_proj_aux_kernel, lay=lay),
            grid=(lay.n_tiles, n // tn), in_specs=in_specs,
            out_specs=[pl.BlockSpec((lay.tm, tn), lambda i, j: (i, j)),
                       pl.BlockSpec((lay.tm, na), lambda i, j: (i, 0))],
            out_shape=[jax.ShapeDtypeStruct((lay.t, n), out_dtype),
                       jax.ShapeDtypeStruct((lay.t, na), F32)],
            scratch_shapes=scratch, compiler_params=params, name=name,
        )(*args, w_aux)
    return pl.pallas_call(
        functools.partial(_proj_kernel, lay=lay),
        grid=(lay.n_tiles, n // tn), in_specs=in_specs,
        out_specs=pl.BlockSpec((lay.tm, tn), lambda i, j: (i, j)),
        out_shape=jax.ShapeDtypeStruct((lay.t, n), out_dtype),
        scratch_shapes=scratch, compiler_params=params, name=name,
    )(*args)


def _out_proj_kernel(a_ref, w_ref, x_ref, gl_ref, gc_ref, o_ref, *, lay):
    y = jnp.dot(a_ref[...], w_ref[...], preferred_element_type=F32)
    is_ctx = _row_in_sample(lay, y.shape) < lay.ctx_len
    gate = jnp.where(is_ctx, gc_ref[0], gl_ref[0])
    o_ref[...] = x_ref[...] + gate * y


def _out_proj(lay, a, w, x, mods, gate_idx, *, tn, name):
    k, d = w.shape
    gl = pl.BlockSpec((1, 1, tn), lambda i, j: (i // TILES_PER_SAMPLE, 0, gate_idx * (d // tn) + j))
    gc = pl.BlockSpec((1, 1, tn), lambda i, j: (lay.batch, 0, gate_idx * (d // tn) + j))
    return pl.pallas_call(
        functools.partial(_out_proj_kernel, lay=lay),
        grid=(lay.n_tiles, d // tn),
        in_specs=[
            pl.BlockSpec((lay.tm, k), lambda i, j: (i, 0)),
            pl.BlockSpec((k, tn), lambda i, j: (0, j)),
            pl.BlockSpec((lay.tm, tn), lambda i, j: (i, j)),
            gl, gc,
        ],
        out_specs=pl.BlockSpec((lay.tm, tn), lambda i, j: (i, j)),
        out_shape=jax.ShapeDtypeStruct((lay.t, d), F32),
        input_output_aliases={2: 0},
        compiler_params=_params("arbitrary", "arbitrary"),
        name=name,
    )(a, w, x, mods, mods)


def _gla_kernel(q_ref, k_ref, v_ref, r_ref, a_ref, waf_ref, baf_ref, wab_ref, bab_ref, nw_ref,
                y_ref, of_ref, ob_ref, sf_ref, sb_ref, *, n_chunks, n_ctx_chunks, dk):
    c_rows = GLA_ROWS
    scale = dk ** -0.5
    ri = lax.broadcasted_iota(jnp.int32, (c_rows, c_rows), 0)
    ci = lax.broadcasted_iota(jnp.int32, (c_rows, c_rows), 1)
    nt = (((1,), (1,)), ((), ()))
    tn_dims = (((0,), (0,)), ((), ()))

    def chunk(c, forward, s_ref):
        rows = pl.ds(pl.multiple_of(c * c_rows, c_rows), c_rows)
        keep = (ci <= ri) if forward else (ci >= ri)
        tri = keep.astype(BF16)
        wa, ba = (waf_ref, baf_ref) if forward else (wab_ref, bab_ref)
        z = jnp.dot(a_ref[rows, :].astype(BF16), wa[...], preferred_element_type=F32) + ba[...]
        g = jnp.maximum(jax.nn.log_sigmoid(z) / GLA_GATE_TAU, GLA_LOG_DECAY_FLOOR)
        g1 = g.astype(BF16)
        rem = g - g1.astype(F32)
        g2 = rem.astype(BF16)
        g3 = (rem - g2.astype(F32)).astype(BF16)
        bcum = (jnp.dot(tri, g1, preferred_element_type=F32)
                + jnp.dot(tri, g2, preferred_element_type=F32)
                + jnp.dot(tri, g3, preferred_element_type=F32))
        mid = bcum[c_rows // 2:c_rows // 2 + 1, :]
        tot = bcum[c_rows - 1:c_rows, :] if forward else bcum[0:1, :]
        q = q_ref[rows, :].astype(F32) * scale
        k = k_ref[rows, :].astype(F32)
        v = v_ref[rows, :]
        q_rel = (q * jnp.exp(bcum - mid)).astype(BF16)
        k_rel = (k * jnp.exp(mid - bcum)).astype(BF16)
        q_dec = (q * jnp.exp(bcum)).astype(BF16)
        k_end = (k * jnp.exp(tot - bcum)).astype(BF16)
        att = lax.dot_general(q_rel, k_rel, nt, preferred_element_type=F32)
        att = jnp.where(keep, att, 0.0).astype(BF16)
        s = s_ref[...]
        o = (jnp.dot(att, v, preferred_element_type=F32)
             + lax.dot_general(q_dec, s.astype(BF16), nt, preferred_element_type=F32))
        s_ref[...] = jnp.exp(tot) * s + lax.dot_general(v, k_end, tn_dims, preferred_element_type=F32)
        return rows, o

    sf_ref[...] = jnp.zeros_like(sf_ref)
    sb_ref[...] = jnp.zeros_like(sb_ref)

    def scan_body(n, carry):
        rows, o = chunk(n, True, sf_ref)
        of_ref[rows, :] = o
        c = jnp.where(n < n_ctx_chunks, n_ctx_chunks - 1 - n, n_chunks - 1 - (n - n_ctx_chunks))
        rows, o = chunk(c, False, sb_ref)
        ob_ref[rows, :] = o
        return carry

    lax.fori_loop(0, n_chunks, scan_body, 0)
    nw = nw_ref[...]

    def out_body(n, carry):
        rows = pl.ds(pl.multiple_of(n * c_rows, c_rows), c_rows)
        o = of_ref[rows, :] + ob_ref[rows, :]
        o = o * lax.rsqrt(jnp.mean(o * o, axis=-1, keepdims=True) + EPS) * nw
        y_ref[rows, :] = (o * jax.nn.silu(r_ref[rows, :].astype(F32))).astype(y_ref.dtype)
        return carry

    lax.fori_loop(0, n_chunks, out_body, 0)


def _gla(lay, qkvr, a, waf, baf, wab, bab, norm_w):
    d = lay.d
    dk = d // 2 // GLA_HEADS
    dv = d // GLA_HEADS
    assert lay.ctx_len % GLA_ROWS == 0 and lay.seq % GLA_ROWS == 0
    n_chunks = lay.p // GLA_ROWS
    kern = functools.partial(_gla_kernel, n_chunks=n_chunks,
                             n_ctx_chunks=lay.ctx_len // GLA_ROWS, dk=dk)
    h = GLA_HEADS
    return pl.pallas_call(
        kern,
        grid=(lay.batch, h),
        in_specs=[
            pl.BlockSpec((lay.p, dk), lambda b, i: (b, i)),
            pl.BlockSpec((lay.p, dk), lambda b, i: (b, h + i)),
            pl.BlockSpec((lay.p, dv), lambda b, i: (b, h + i)),
            pl.BlockSpec((lay.p, dv), lambda b, i: (b, 2 * h + i)),
            pl.BlockSpec((lay.p, LANES), lambda b, i: (b, 0)),
            pl.BlockSpec((LANES, dk), lambda b, i: (0, i)),
            pl.BlockSpec((1, dk), lambda b, i: (0, i)),
            pl.BlockSpec((LANES, dk), lambda b, i: (0, i)),
            pl.BlockSpec((1, dk), lambda b, i: (0, i)),
            pl.BlockSpec((1, dv), lambda b, i: (0, 0)),
        ],
        out_specs=pl.BlockSpec((lay.p, dv), lambda b, i: (b, i)),
        out_shape=jax.ShapeDtypeStruct((lay.t, d), BF16),
        scratch_shapes=[pltpu.VMEM((lay.p, dv), F32), pltpu.VMEM((lay.p, dv), F32),
                        pltpu.VMEM((dv, dk), F32), pltpu.VMEM((dv, dk), F32)],
        compiler_params=_params("arbitrary", "arbitrary"),
        name="gla_scan",
    )(qkvr, qkvr, qkvr, qkvr, a, waf, baf, wab, bab, norm_w)


def _route(logits, bias):
    n_rows = logits.shape[0]
    per_group = N_EXPERTS // N_GROUPS
    lane_i = lax.broadcasted_iota(jnp.int32, (n_rows, LANES), 1)
    lane = lane_i.astype(F32)
    group = (lane_i // per_group).astype(F32)
    valid = lane_i < N_EXPERTS
    big = float(2 * LANES)
    scores = jax.nn.sigmoid(logits)
    sel = jnp.where(valid, scores + bias, NEG_INF)

    def partner(x, s):
        return jnp.where((lane_i & s) == 0, pltpu.roll(x, LANES - s, axis=1), pltpu.roll(x, s, axis=1))

    def group_reduce(x, op):
        s = 1
        while s < per_group:
            x = op(x, partner(x, s))
            s *= 2
        return x

    m1 = group_reduce(sel, jnp.maximum)
    first = group_reduce(jnp.where(sel == m1, lane, big), jnp.minimum)
    m2 = group_reduce(jnp.where(lane == first, NEG_INF, sel), jnp.maximum)
    cur = jnp.where(valid, m1 + m2, NEG_INF)

    def pick_best(cur, ident):
        m = jnp.max(cur, axis=1, keepdims=True)
        best = jnp.min(jnp.where(cur == m, ident, big), axis=1, keepdims=True)
        return ident == best

    group_ok = jnp.zeros((n_rows, LANES), jnp.bool_)
    for _ in range(TOPK_GROUPS):
        p = pick_best(cur, group)
        group_ok = group_ok | p
        cur = jnp.where(p, NEG_INF, cur)

    cur = jnp.where(group_ok & valid, sel, NEG_INF)
    picked = jnp.zeros((n_rows, LANES), jnp.bool_)
    ids = jnp.zeros((n_rows, LANES), F32)
    for it in range(TOP_K):
        m = jnp.max(cur, axis=1, keepdims=True)
        best = jnp.min(jnp.where(cur == m, lane, big), axis=1, keepdims=True)
        p = lane == best
        picked = picked | p
        cur = jnp.where(p, NEG_INF, cur)
        ids = jnp.where(lane_i == it, best, ids)

    w = jnp.where(picked, scores, 0.0)
    gates = w / jnp.sum(w, axis=1, keepdims=True) * ROUTED_SCALE
    return gates, ids.astype(jnp.int32)


def _router_kernel(x_ref, nw_ref, shl, scl, shc, scc, rw_ref, rb_ref, hp_ref, g_ref, id_ref, h_ref, *, lay):
    _modnorm_rows(lay, x_ref, nw_ref, shl, scl, shc, scc, h_ref, packed_ref=hp_ref)
    logits = jnp.dot(h_ref[...], rw_ref[...], preferred_element_type=F32)
    g_ref[...], id_ref[...] = _route(logits, rb_ref[...])


def _router(lay, x, nw, mods, rw, rb):
    d = lay.d
    shl, shc = _mod_specs(lay, 3, two_axes=False)
    scl, scc = _mod_specs(lay, 4, two_axes=False)
    return pl.pallas_call(
        functools.partial(_router_kernel, lay=lay),
        grid=(lay.n_tiles,),
        in_specs=[
            pl.BlockSpec((lay.tm, d), lambda i: (i, 0)),
            pl.BlockSpec((1, d), lambda i: (0, 0)),
            shl, scl, shc, scc,
            pl.BlockSpec((d, LANES), lambda i: (0, 0)),
            pl.BlockSpec((1, LANES), lambda i: (0, 0)),
        ],
        out_specs=[pl.BlockSpec((lay.tm, d // 2), lambda i: (i, 0)),
                   pl.BlockSpec((lay.tm, LANES), lambda i: (i, 0)),
                   pl.BlockSpec((lay.tm, LANES), lambda i: (i, 0))],
        out_shape=[jax.ShapeDtypeStruct((lay.t, d // 2), U32),
                   jax.ShapeDtypeStruct((lay.t, LANES), F32),
                   jax.ShapeDtypeStruct((lay.t, LANES), jnp.int32)],
        scratch_shapes=[pltpu.VMEM((lay.tm, d), BF16)],
        compiler_params=_params("arbitrary"),
        name="moe_router",
    )(x, nw, mods, mods, mods, mods, rw, rb)


def _moe_plan(lay, gates, ids):
    t, rb = lay.t, MOE_ROWS
    n_pairs = t * TOP_K
    n_blocks = n_pairs // rb + N_EXPERTS
    assert N_EXPERTS * n_pairs < 2 ** 31
    eidx = ids[:, :TOP_K]
    w = jnp.take_along_axis(gates, eidx, axis=1).reshape(-1)
    key = eidx.reshape(-1) * n_pairs + jnp.arange(n_pairs, dtype=jnp.int32)
    key_s, w_s = lax.sort((key, w), num_keys=1)
    bounds = jnp.searchsorted(key_s, jnp.arange(N_EXPERTS + 1, dtype=jnp.int32) * n_pairs).astype(jnp.int32)
    start, counts = bounds[:-1], bounds[1:] - bounds[:-1]
    padded = (counts + rb - 1) // rb * rb
    pad_end = jnp.cumsum(padded)
    pad_start = pad_end - padded
    blk = jnp.arange(n_blocks + MOE_EXTRA_BLOCKS, dtype=jnp.int32) - 1
    e_blk = jnp.clip(jnp.searchsorted(pad_end, blk * rb, side="right"), 0, N_EXPERTS - 1)
    r = jnp.arange(rb, dtype=jnp.int32)[None, :]
    in_run = (blk * rb - pad_start[e_blk])[:, None] + r
    valid = ((blk >= 0) & (blk * rb < pad_end[-1]))[:, None] & (in_run < counts[e_blk][:, None])
    src = jnp.clip(start[e_blk][:, None] + in_run, 0, n_pairs - 1)
    pair = key_s[src] % n_pairs
    tok = jnp.where(valid, pair // TOP_K, r)
    dst = jnp.where(valid, (pair % TOP_K) * t + pair // TOP_K, TOP_K * t + (blk % 2)[:, None] * rb + r)
    gate = jnp.where(valid, w_s[src], 0.0)[1:n_blocks + 1].reshape(n_blocks * rb, 1)
    n_used = pad_end[-1] // rb
    step_blk = jnp.minimum(jnp.arange(n_blocks + 1, dtype=jnp.int32), n_used - 1)
    block_expert = jnp.searchsorted(pad_end, step_blk * rb, side="right").astype(jnp.int32)
    return tok.astype(jnp.int32), dst.astype(jnp.int32), gate, block_expert


def _moe_expert_kernel(be_ref, tok_hbm, dst_hbm, g_ref, win_ref, wout_ref, h_hbm, y_hbm,
                       hbuf, ybuf, winb, woutb, tok_smem, dst_smem, gsem, ssem, isem,
                       *, t_rows, n_blocks, f):
    rb = MOE_ROWS
    nb = pl.program_id(0)
    ys = nb % 2
    hs = nb % 3
    half = hbuf.shape[-1]
    dump0 = TOP_K * t_rows

    def ids_copies(block):
        k = (block + 1) % MOE_ID_RING
        return (pltpu.make_async_copy(tok_hbm.at[block + 1], tok_smem.at[k], isem.at[0]),
                pltpu.make_async_copy(dst_hbm.at[block + 1], dst_smem.at[k], isem.at[1]))

    def gather_row(k, r, s):
        return pltpu.make_async_copy(h_hbm.at[pl.ds(tok_smem[k, r], 1)], hbuf.at[s, pl.ds(r, 1)],
                                     gsem.at[s])

    def scatter_row(k, r, s):
        return pltpu.make_async_copy(ybuf.at[s, pl.ds(r, 1)], y_hbm.at[pl.ds(dst_smem[k, r], 1)],
                                     ssem.at[s])

    def gather_wait(s):
        pltpu.make_async_copy(h_hbm.at[pl.ds(0, rb)], hbuf.at[s], gsem.at[s]).wait()

    def scatter_wait(s):
        pltpu.make_async_copy(ybuf.at[s], y_hbm.at[pl.ds(0, rb)], ssem.at[s]).wait()

    @pl.when(nb == 0)
    def _():
        for block in (-1, 0, 1, 2):
            for c in ids_copies(block):
                c.start()
                c.wait()
        ybuf[...] = jnp.zeros_like(ybuf)
        pltpu.make_async_copy(ybuf.at[0], y_hbm.at[pl.ds(dump0, rb)], ssem.at[0]).start()
        for block in (0, 1):
            def body(r, carry, block=block):
                gather_row(block + 1, r, block).start()
                return carry
            lax.fori_loop(0, rb, body, 0)

    @pl.when(nb > 0)
    def _():
        for c in ids_copies(nb + 2):
            c.wait()

    for c in ids_copies(nb + 3):
        c.start()

    @pl.when((nb == 0) | (be_ref[nb] != be_ref[jnp.maximum(nb - 1, 0)]))
    def _():
        winb[...] = win_ref[0].astype(BF16)
        woutb[...] = wout_ref[0].astype(BF16)

    gk = (nb + 3) % MOE_ID_RING
    sk = nb % MOE_ID_RING
    ghs = (nb + 2) % 3
    n_groups = 8
    per_group = rb // n_groups

    def issue(group):
        rows = range(group * per_group, (group + 1) * per_group)
        copies = [gather_row(gk, r, ghs) for r in rows] + [scatter_row(sk, r, 1 - ys) for r in rows]
        for i, c in enumerate(copies):
            c.start(priority=i % 2)

    gather_wait(hs)
    issue(0)
    cols = half // 4
    hu = jnp.zeros((rb, 2 * f), F32)
    for piece in range(4):
        c0 = piece * cols
        lo, hi = _unpack_halves(hbuf[hs, :, c0:c0 + cols])
        hu = (hu + jnp.dot(lo.astype(BF16), winb[c0:c0 + cols, :], preferred_element_type=F32)
              + jnp.dot(hi.astype(BF16), winb[half + c0:half + c0 + cols, :],
                        preferred_element_type=F32))
        issue(1 + piece)
    act = (jax.nn.silu(hu[:, :f]) * hu[:, f:] * g_ref[...]).astype(BF16)
    scatter_wait(ys)
    for piece in range(4):
        c0 = piece * cols
        y_lo = jnp.dot(act, woutb[:, c0:c0 + cols], preferred_element_type=F32)
        y_hi = jnp.dot(act, woutb[:, half + c0:half + c0 + cols], preferred_element_type=F32)
        ybuf[ys, :, c0:c0 + cols] = _pack_halves(jnp.concatenate([y_lo, y_hi], axis=1))
        if piece < 3:
            issue(5 + piece)

    @pl.when(nb == n_blocks)
    def _():
        scatter_wait(1 - ys)
        gather_wait((nb + 1) % 3)
        gather_wait((nb + 2) % 3)
        for c in ids_copies(nb + 3):
            c.wait()


def _moe_experts(lay, hp, tok, dst, gate, block_expert, w_in, w_out):
    ne, d, f2 = w_in.shape
    f = f2 // 2
    rb = MOE_ROWS
    half = d // 2
    n_blocks = gate.shape[0] // rb
    assert tok.shape == dst.shape == (n_blocks + MOE_EXTRA_BLOCKS, rb) and rb <= lay.t
    kern = functools.partial(_moe_expert_kernel, t_rows=lay.t, n_blocks=n_blocks, f=f)
    last = n_blocks - 1
    grid_spec = pltpu.PrefetchScalarGridSpec(
        num_scalar_prefetch=1,
        grid=(n_blocks + 1,),
        in_specs=[
            pl.BlockSpec(memory_space=pl.ANY),
            pl.BlockSpec(memory_space=pl.ANY),
            pl.BlockSpec((rb, 1), lambda nb, be: (jnp.minimum(nb, last), 0)),
            pl.BlockSpec((1, d, f2), lambda nb, be: (be[nb], 0, 0)),
            pl.BlockSpec((1, f, d), lambda nb, be: (be[nb], 0, 0)),
            pl.BlockSpec(memory_space=pl.ANY),
        ],
        out_specs=pl.BlockSpec(memory_space=pl.ANY),
        scratch_shapes=[
            pltpu.VMEM((3, rb, half), U32),
            pltpu.VMEM((2, rb, half), U32),
            pltpu.VMEM((d, f2), BF16),
            pltpu.VMEM((f, d), BF16),
            pltpu.SMEM((MOE_ID_RING, rb), jnp.int32),
            pltpu.SMEM((MOE_ID_RING, rb), jnp.int32),
            pltpu.SemaphoreType.DMA((3,)),
            pltpu.SemaphoreType.DMA((2,)),
            pltpu.SemaphoreType.DMA((2,)),
        ],
    )
    return pl.pallas_call(
        kern,
        grid_spec=grid_spec,
        out_shape=jax.ShapeDtypeStruct((TOP_K * lay.t + 2 * rb, half), U32),
        compiler_params=_params("arbitrary"),
        name="moe_experts",
    )(block_expert, tok, dst, gate, w_in, w_out, hp)


def _moe_combine_kernel(*refs, lay, f):
    y_refs = refs[:TOP_K]
    h_ref, wsi_ref, wso_ref, x_ref, gl_ref, gc_ref, o_ref = refs[TOP_K:]
    half = h_ref.shape[-1]
    lo, hi = _unpack_halves(y_refs[0][...])
    for y_ref in y_refs[1:]:
        l, h = _unpack_halves(y_ref[...])
        lo, hi = lo + l, hi + h
    hlo, hhi = _unpack_halves(h_ref[...])
    hu = (jnp.dot(hlo.astype(BF16), wsi_ref[:half, :], preferred_element_type=F32)
          + jnp.dot(hhi.astype(BF16), wsi_ref[half:, :], preferred_element_type=F32))
    act = (jax.nn.silu(hu[:, :f]) * hu[:, f:]).astype(BF16)
    shared = jnp.dot(act, wso_ref[...], preferred_element_type=F32)
    is_ctx = (pl.program_id(0) * COMBINE_ROWS) % lay.p < lay.ctx_len
    gate = jnp.where(is_ctx, gc_ref[0], gl_ref[0])
    o_ref[:, :half] = x_ref[:, :half] + gate[:, :half] * (lo + shared[:, :half])
    o_ref[:, half:] = x_ref[:, half:] + gate[:, half:] * (hi + shared[:, half:])


def _moe_combine(lay, y, hp, ws_in, ws_out, x, mods, gate_idx):
    d = lay.d
    half = d // 2
    rt = COMBINE_ROWS
    assert lay.ctx_len % rt == 0 and lay.p % rt == 0
    steps = lay.t // rt
    f = ws_out.shape[0]
    y_specs = [pl.BlockSpec((rt, half), functools.partial(lambda i, k: (k * steps + i, 0), k=k))
               for k in range(TOP_K)]
    return pl.pallas_call(
        functools.partial(_moe_combine_kernel, lay=lay, f=f),
        grid=(steps,),
        in_specs=y_specs + [
            pl.BlockSpec((rt, half), lambda i: (i, 0)),
            pl.BlockSpec((d, 2 * f), lambda i: (0, 0)),
            pl.BlockSpec((f, d), lambda i: (0, 0)),
            pl.BlockSpec((rt, d), lambda i: (i, 0)),
            pl.BlockSpec((1, 1, d), lambda i: (i * rt // lay.p, 0, gate_idx)),
            pl.BlockSpec((1, 1, d), lambda i: (lay.batch, 0, gate_idx)),
        ],
        out_specs=pl.BlockSpec((rt, d), lambda i: (i, 0)),
        out_shape=jax.ShapeDtypeStruct((lay.t, d), F32),
        input_output_aliases={TOP_K + 3: 0},
        compiler_params=_params("arbitrary"),
        name="moe_combine",
    )(*([y] * TOP_K), hp, ws_in, ws_out, x, mods, mods)


def _final_norm_kernel(x_ref, w_ref, o_ref):
    x = x_ref[0]
    o_ref[0] = x * lax.rsqrt(jnp.mean(x * x, axis=-1, keepdims=True) + EPS) * w_ref[...]


def _final_norm(lay, x, w):
    d = lay.d
    rows = _pick(lay.seq, (256, 128, 64))
    assert lay.ctx_len % rows == 0
    off = lay.ctx_len // rows
    return pl.pallas_call(
        _final_norm_kernel,
        grid=(lay.batch, lay.seq // rows),
        in_specs=[pl.BlockSpec((1, rows, d), lambda b, j: (b, off + j, 0)),
                  pl.BlockSpec((1, d), lambda b, j: (0, 0))],
        out_specs=pl.BlockSpec((1, rows, d), lambda b, j: (b, j, 0)),
        out_shape=jax.ShapeDtypeStruct((lay.batch, lay.seq, d), F32),
        compiler_params=_params("arbitrary", "arbitrary"),
        name="final_norm",
    )(x.reshape(lay.batch, lay.p, d), w)


def kernel(x, c, ctx, c_ctx, ada_w_down, ada_w_up, ada_b, norm_mix, norm_ffn, conv_w_in, conv_w, conv_w_out, gla_w_in, gla_w_a2_fwd, gla_b_a2_fwd, gla_w_a2_bwd, gla_b_a2_bwd, gla_norm, gla_w_out, router_w, router_bias, exp_w_in, exp_w_out, shared_w_in, shared_w_out, norm_final):
    batch, seq, d = x.shape
    ctx_len = ctx.shape[1]
    depth = ada_w_down.shape[0]
    lay = _Layout(batch, ctx_len, seq, d)
    qk = d // 2
    tn = _pick(d, (512, 256, 128))

    cond = jnp.zeros((MOD_ROWS, d), F32).at[:batch].set(c).at[batch].set(c_ctx)
    mods_all = _ada_all(cond, ada_w_down, ada_w_up, ada_b)
    xs = jnp.concatenate([ctx, x], axis=1).reshape(lay.t, d)

    for i in range(depth):
        j = i // 2
        mods = mods_all[i].reshape(MOD_ROWS, 1, N_MOD * d)
        nw_mix = norm_mix[i].reshape(1, d)
        if i % 2 == 0:
            y = _norm_proj(lay, xs, nw_mix, mods, 0, 1, conv_w_in[j].astype(BF16), out_dtype=BF16,
                           tn=_pick(d, (256, 128)), conv_w=conv_w[j], name="conv_in")
            xs = _out_proj(lay, y, conv_w_out[j].astype(BF16), xs, mods, 2, tn=tn, name="conv_out")
        else:
            n_main = 2 * qk + 2 * d
            w_main = gla_w_in[j][:, :n_main].astype(BF16)
            w_aux = jnp.zeros((d, LANES), BF16).at[:, :2 * GLA_GATE_RANK].set(
                gla_w_in[j][:, n_main:].astype(BF16))
            qkvr, a = _norm_proj(lay, xs, nw_mix, mods, 0, 1, w_main, out_dtype=BF16, tn=tn,
                                 w_aux=w_aux, name="gla_in")
            waf = jnp.zeros((LANES, qk), BF16).at[:GLA_GATE_RANK].set(gla_w_a2_fwd[j].astype(BF16))
            wab = jnp.zeros((LANES, qk), BF16).at[GLA_GATE_RANK:2 * GLA_GATE_RANK].set(
                gla_w_a2_bwd[j].astype(BF16))
            y = _gla(lay, qkvr, a, waf, gla_b_a2_fwd[j].reshape(1, qk), wab,
                     gla_b_a2_bwd[j].reshape(1, qk), gla_norm[j].reshape(1, d // GLA_HEADS))
            xs = _out_proj(lay, y, gla_w_out[j].astype(BF16), xs, mods, 2, tn=tn, name="gla_out")

        rw = jnp.zeros((d, LANES), BF16).at[:, :N_EXPERTS].set(router_w[i].astype(BF16))
        rb = jnp.zeros((1, LANES), F32).at[0, :N_EXPERTS].set(router_bias[i])
        hp, gates, ids = _router(lay, xs, norm_ffn[i].reshape(1, d), mods, rw, rb)
        tok, dst, gate, block_expert = _moe_plan(lay, gates, ids)
        y = _moe_experts(lay, hp, tok, dst, gate, block_expert, exp_w_in[i], exp_w_out[i])
        xs = _moe_combine(lay, y, hp, shared_w_in[i].astype(BF16), shared_w_out[i].astype(BF16),
                          xs, mods, 5)

    return _final_norm(lay, xs, norm_final.reshape(1, d))
```

```python
import functools

import jax
import jax.numpy as jnp
import numpy as np
from jax import lax
from jax.experimental import pallas as pl
from jax.experimental.pallas import tpu as pltpu

GRID_W = 64
EPS = 1e-6
N_MOD = 6
GLA_HEADS = 8
GLA_GATE_RANK = 16
GLA_GATE_TAU = 16.0
GLA_LOG_DECAY_FLOOR = -1.0
N_EXPERTS = 64
N_GROUPS = 8
TOPK_GROUPS = 4
TOP_K = 8
ROUTED_SCALE = 2.5

LANES = 128
GLA_ROWS = 128
TILES_PER_SAMPLE = 4
MOD_ROWS = 16
VMEM_LIMIT_BYTES = 56 * 1024 * 1024

MOE_ROWS = 256
COMBINE_ROWS = 128
MOE_EXTRA_BLOCKS = 5
MOE_ID_RING = 8

BF16 = jnp.bfloat16
F32 = jnp.float32
U32 = jnp.uint32
HIGH_HALF = np.uint32(0xFFFF0000)
NEG_INF = float("-inf")


def _params(*sem):
    return pltpu.CompilerParams(dimension_semantics=sem, vmem_limit_bytes=VMEM_LIMIT_BYTES)


def _pick(n, prefs):
    for p in prefs:
        if n % p == 0:
            return p
    return n


def _ada_kernel(cond_ref, wd_ref, wu_ref, b_ref, o_ref, z_ref):
    @pl.when(pl.program_id(1) == 0)
    def _():
        s = jax.nn.silu(cond_ref[...]).astype(BF16)
        z_ref[...] = jnp.dot(s, wd_ref[0].astype(BF16), preferred_element_type=F32)

    o_ref[0] = jnp.dot(z_ref[...].astype(BF16), wu_ref[0].astype(BF16),
                       preferred_element_type=F32) + b_ref[0]


def _ada_all(cond, w_down, w_up, b):
    depth, d, r = w_down.shape
    n = w_up.shape[-1]
    tn = _pick(n, (2048, 1024, 512, 256, 128))
    return pl.pallas_call(
        _ada_kernel,
        grid=(depth, n // tn),
        in_specs=[
            pl.BlockSpec((MOD_ROWS, d), lambda l, j: (0, 0)),
            pl.BlockSpec((1, d, r), lambda l, j: (l, 0, 0)),
            pl.BlockSpec((1, r, tn), lambda l, j: (l, 0, j)),
            pl.BlockSpec((1, 1, tn), lambda l, j: (l, 0, j)),
        ],
        out_specs=pl.BlockSpec((1, MOD_ROWS, tn), lambda l, j: (l, 0, j)),
        out_shape=jax.ShapeDtypeStruct((depth, MOD_ROWS, n), F32),
        scratch_shapes=[pltpu.VMEM((MOD_ROWS, r), F32)],
        compiler_params=_params("arbitrary", "arbitrary"),
        name="ada_mod",
    )(cond, w_down, w_up, b.reshape(depth, 1, n))


class _Layout:
    def __init__(self, batch, ctx_len, seq, d):
        self.batch, self.ctx_len, self.seq, self.d = batch, ctx_len, seq, d
        self.p = ctx_len + seq
        self.t = batch * self.p
        assert self.p % TILES_PER_SAMPLE == 0
        self.tm = self.p // TILES_PER_SAMPLE
        assert GRID_W & (GRID_W - 1) == 0
        assert self.tm % GRID_W == 0 and ctx_len % GRID_W == 0 and self.tm % 16 == 0
        assert ctx_len <= self.tm and batch < MOD_ROWS
        self.n_tiles = self.t // self.tm
        self.slab = _pick(self.tm, (32, 16))
        assert ctx_len % self.slab == 0


def _mod_specs(lay, idx, two_axes=True):
    d = lay.d
    if two_axes:
        lat = pl.BlockSpec((1, 1, d), lambda i, j: (i // TILES_PER_SAMPLE, 0, idx))
        ctx = pl.BlockSpec((1, 1, d), lambda i, j: (lay.batch, 0, idx))
    else:
        lat = pl.BlockSpec((1, 1, d), lambda i: (i // TILES_PER_SAMPLE, 0, idx))
        ctx = pl.BlockSpec((1, 1, d), lambda i: (lay.batch, 0, idx))
    return lat, ctx


def _tile_row0(lay):
    return (pl.program_id(0) % TILES_PER_SAMPLE) * lay.tm


def _row_in_sample(lay, shape):
    return _tile_row0(lay) + lax.broadcasted_iota(jnp.int32, shape, 0)


def _pack_halves(x):
    half = x.shape[1] // 2
    xb = x.astype(BF16).astype(F32)
    lo = lax.bitcast_convert_type(xb[:, :half], U32) >> 16
    hi = lax.bitcast_convert_type(xb[:, half:], U32) & HIGH_HALF
    return hi | lo


def _unpack_halves(w):
    lo = lax.bitcast_convert_type(w << 16, F32)
    hi = lax.bitcast_convert_type(w & HIGH_HALF, F32)
    return lo, hi


def _modnorm_rows(lay, x_ref, nw_ref, shl_ref, scl_ref, shc_ref, scc_ref, h_ref, packed_ref=None):
    nw = nw_ref[...]
    slab = lay.slab
    row0 = _tile_row0(lay)

    def body(s, carry):
        r = pl.multiple_of(s * slab, slab)
        x = x_ref[pl.ds(r, slab), :]
        ms = jnp.mean(x * x, axis=-1, keepdims=True)
        y = x * lax.rsqrt(ms + EPS) * nw
        is_ctx = row0 + r < lay.ctx_len
        sc = jnp.where(is_ctx, scc_ref[0], scl_ref[0])
        sh = jnp.where(is_ctx, shc_ref[0], shl_ref[0])
        h = y * (1.0 + sc) + sh
        h_ref[pl.ds(r, slab), :] = h.astype(h_ref.dtype)
        if packed_ref is not None:
            packed_ref[pl.ds(r, slab), :] = _pack_halves(h)
        return carry

    lax.fori_loop(0, lay.tm // slab, body, 0)


def _proj_kernel(x_ref, nw_ref, shl, scl, shc, scc, w_ref, o_ref, h_ref, *, lay):
    @pl.when(pl.program_id(1) == 0)
    def _():
        _modnorm_rows(lay, x_ref, nw_ref, shl, scl, shc, scc, h_ref)

    o_ref[...] = jnp.dot(h_ref[...], w_ref[...], preferred_element_type=F32).astype(o_ref.dtype)


def _proj_aux_kernel(x_ref, nw_ref, shl, scl, shc, scc, w_ref, wa_ref, o_ref, a_ref, h_ref, *, lay):
    @pl.when(pl.program_id(1) == 0)
    def _():
        _modnorm_rows(lay, x_ref, nw_ref, shl, scl, shc, scc, h_ref)
        a_ref[...] = jnp.dot(h_ref[...], wa_ref[...], preferred_element_type=F32)

    o_ref[...] = jnp.dot(h_ref[...], w_ref[...], preferred_element_type=F32).astype(o_ref.dtype)


def _conv_proj_kernel(x_ref, nw_ref, shl, scl, shc, scc, wb_ref, wc_ref, wv_ref, cw_ref, o_ref, h_ref,
                      *, lay, tc):
    @pl.when(pl.program_id(1) == 0)
    def _():
        _modnorm_rows(lay, x_ref, nw_ref, shl, scl, shc, scc, h_ref)

    tm = lay.tm
    h = h_ref[...]
    bg = jnp.dot(h, wb_ref[...], preferred_element_type=F32)
    cg = jnp.dot(h, wc_ref[...], preferred_element_type=F32)
    v = jnp.dot(h, wv_ref[...], preferred_element_type=F32)
    u = cg * v
    row = _row_in_sample(lay, (tm, tc))
    is_ctx = row < lay.ctx_len
    col = (row - lay.ctx_len) & (GRID_W - 1)
    first = (is_ctx & (row == 0)) | (~is_ctx & (col == 0))
    last = (is_ctx & (row == lay.ctx_len - 1)) | (~is_ctx & (col == GRID_W - 1))
    prev = jnp.where(first, 0.0, pltpu.roll(u, 1, axis=0))
    nxt = jnp.where(last, 0.0, pltpu.roll(u, tm - 1, axis=0))
    cw = cw_ref[...]
    conv = prev * cw[0:1, :] + u * cw[1:2, :] + nxt * cw[2:3, :]
    o_ref[...] = (bg * conv).astype(o_ref.dtype)


def _norm_proj(lay, x, nw, mods, shift_idx, scale_idx, w, *, out_dtype, tn, n=None, w_aux=None,
               conv_w=None, name):
    d = w.shape[0]
    n = w.shape[1] if n is None else n
    assert n % tn == 0
    shl, shc = _mod_specs(lay, shift_idx)
    scl, scc = _mod_specs(lay, scale_idx)
    in_specs = [
        pl.BlockSpec((lay.tm, d), lambda i, j: (i, 0)),
        pl.BlockSpec((1, d), lambda i, j: (0, 0)),
        shl, scl, shc, scc,
        pl.BlockSpec((d, tn), lambda i, j: (0, j)),
    ]
    args = [x, nw, mods, mods, mods, mods, w]
    scratch = [pltpu.VMEM((lay.tm, d), BF16)]
    params = _params("arbitrary", "arbitrary")
    if conv_w is not None:
        tc = tn
        nct = n // 3 // tc
        in_specs[-1:] = [pl.BlockSpec((d, tc), functools.partial(lambda i, j, k: (0, k * nct + j), k=k))
                         for k in range(3)]
        in_specs.append(pl.BlockSpec((3, tc), lambda i, j: (0, j)))
        return pl.pallas_call(
            functools.partial(_conv_proj_kernel, lay=lay, tc=tc),
            grid=(lay.n_tiles, nct), in_specs=in_specs,
            out_specs=pl.BlockSpec((lay.tm, tc), lambda i, j: (i, j)),
            out_shape=jax.ShapeDtypeStruct((lay.t, n // 3), out_dtype),
            scratch_shapes=scratch, compiler_params=params, name=name,
        )(*args[:-1], w, w, w, conv_w)
    if w_aux is not None:
        na = w_aux.shape[1]
        in_specs.append(pl.BlockSpec((d, na), lambda i, j: (0, 0)))
        return pl.pallas_call(
            functools.partial(_proj_aux_kernel, lay=lay),
            grid=(lay.n_tiles, n // tn), in_specs=in_specs,
            out_specs=[pl.BlockSpec((lay.tm, tn), lambda i, j: (i, j)),
                       pl.BlockSpec((lay.tm, na), lambda i, j: (i, 0))],
            out_shape=[jax.ShapeDtypeStruct((lay.t, n), out_dtype),
                       jax.ShapeDtypeStruct((lay.t, na), F32)],
            scratch_shapes=scratch, compiler_params=params, name=name,
        )(*args, w_aux)
    return pl.pallas_call(
        functools.partial(_proj_kernel, lay=lay),
        grid=(lay.n_tiles, n // tn), in_specs=in_specs,
        out_specs=pl.BlockSpec((lay.tm, tn), lambda i, j: (i, j)),
        out_shape=jax.ShapeDtypeStruct((lay.t, n), out_dtype),
        scratch_shapes=scratch, compiler_params=params, name=name,
    )(*args)


def _out_proj_kernel(a_ref, w_ref, x_ref, gl_ref, gc_ref, o_ref, *, lay):
    y = jnp.dot(a_ref[...], w_ref[...], preferred_element_type=F32)
    is_ctx = _row_in_sample(lay, y.shape) < lay.ctx_len
    gate = jnp.where(is_ctx, gc_ref[0], gl_ref[0])
    o_ref[...] = x_ref[...] + gate * y


def _out_proj(lay, a, w, x, mods, gate_idx, *, tn, name):
    k, d = w.shape
    gl = pl.BlockSpec((1, 1, tn), lambda i, j: (i // TILES_PER_SAMPLE, 0, gate_idx * (d // tn) + j))
    gc = pl.BlockSpec((1, 1, tn), lambda i, j: (lay.batch, 0, gate_idx * (d // tn) + j))
    return pl.pallas_call(
        functools.partial(_out_proj_kernel, lay=lay),
        grid=(lay.n_tiles, d // tn),
        in_specs=[
            pl.BlockSpec((lay.tm, k), lambda i, j: (i, 0)),
            pl.BlockSpec((k, tn), lambda i, j: (0, j)),
            pl.BlockSpec((lay.tm, tn), lambda i, j: (i, j)),
            gl, gc,
        ],
        out_specs=pl.BlockSpec((lay.tm, tn), lambda i, j: (i, j)),
        out_shape=jax.ShapeDtypeStruct((lay.t, d), F32),
        input_output_aliases={2: 0},
        compiler_params=_params("arbitrary", "arbitrary"),
        name=name,
    )(a, w, x, mods, mods)


def _gla_kernel(q_ref, k_ref, v_ref, r_ref, a_ref, waf_ref, baf_ref, wab_ref, bab_ref, nw_ref,
                y_ref, of_ref, ob_ref, sf_ref, sb_ref, *, n_chunks, n_ctx_chunks, dk):
    c_rows = GLA_ROWS
    scale = dk ** -0.5
    ri = lax.broadcasted_iota(jnp.int32, (c_rows, c_rows), 0)
    ci = lax.broadcasted_iota(jnp.int32, (c_rows, c_rows), 1)
    nt = (((1,), (1,)), ((), ()))
    tn_dims = (((0,), (0,)), ((), ()))

    def chunk(c, forward, s_ref):
        rows = pl.ds(pl.multiple_of(c * c_rows, c_rows), c_rows)
        keep = (ci <= ri) if forward else (ci >= ri)
        tri = keep.astype(BF16)
        wa, ba = (waf_ref, baf_ref) if forward else (wab_ref, bab_ref)
        z = jnp.dot(a_ref[rows, :].astype(BF16), wa[...], preferred_element_type=F32) + ba[...]
        g = jnp.maximum(jax.nn.log_sigmoid(z) / GLA_GATE_TAU, GLA_LOG_DECAY_FLOOR)
        g1 = g.astype(BF16)
        rem = g - g1.astype(F32)
        g2 = rem.astype(BF16)
        g3 = (rem - g2.astype(F32)).astype(BF16)
        bcum = (jnp.dot(tri, g1, preferred_element_type=F32)
                + jnp.dot(tri, g2, preferred_element_type=F32)
                + jnp.dot(tri, g3, preferred_element_type=F32))
        mid = bcum[c_rows // 2:c_rows // 2 + 1, :]
        tot = bcum[c_rows - 1:c_rows, :] if forward else bcum[0:1, :]
        q = q_ref[rows, :].astype(F32) * scale
        k = k_ref[rows, :].astype(F32)
        v = v_ref[rows, :]
        q_rel = (q * jnp.exp(bcum - mid)).astype(BF16)
        k_rel = (k * jnp.exp(mid - bcum)).astype(BF16)
        q_dec = (q * jnp.exp(bcum)).astype(BF16)
        k_end = (k * jnp.exp(tot - bcum)).astype(BF16)
        att = lax.dot_general(q_rel, k_rel, nt, preferred_element_type=F32)
        att = jnp.where(keep, att, 0.0).astype(BF16)
        s = s_ref[...]
        o = (jnp.dot(att, v, preferred_element_type=F32)
             + lax.dot_general(q_dec, s.astype(BF16), nt, preferred_element_type=F32))
        s_ref[...] = jnp.exp(tot) * s + lax.dot_general(v, k_end, tn_dims, preferred_element_type=F32)
        return rows, o

    sf_ref[...] = jnp.zeros_like(sf_ref)
    sb_ref[...] = jnp.zeros_like(sb_ref)

    def scan_body(n, carry):
        rows, o = chunk(n, True, sf_ref)
        of_ref[rows, :] = o
        c = jnp.where(n < n_ctx_chunks, n_ctx_chunks - 1 - n, n_chunks - 1 - (n - n_ctx_chunks))
        rows, o = chunk(c, False, sb_ref)
        ob_ref[rows, :] = o
        return carry

    lax.fori_loop(0, n_chunks, scan_body, 0)
    nw = nw_ref[...]

    def out_body(n, carry):
        rows = pl.ds(pl.multiple_of(n * c_rows, c_rows), c_rows)
        o = of_ref[rows, :] + ob_ref[rows, :]
        o = o * lax.rsqrt(jnp.mean(o * o, axis=-1, keepdims=True) + EPS) * nw
        y_ref[rows, :] = (o * jax.nn.silu(r_ref[rows, :].astype(F32))).astype(y_ref.dtype)
        return carry

    lax.fori_loop(0, n_chunks, out_body, 0)


def _gla(lay, qkvr, a, waf, baf, wab, bab, norm_w):
    d = lay.d
    dk = d // 2 // GLA_HEADS
    dv = d // GLA_HEADS
    assert lay.ctx_len % GLA_ROWS == 0 and lay.seq % GLA_ROWS == 0
    n_chunks = lay.p // GLA_ROWS
    kern = functools.partial(_gla_kernel, n_chunks=n_chunks,
                             n_ctx_chunks=lay.ctx_len // GLA_ROWS, dk=dk)
    h = GLA_HEADS
    return pl.pallas_call(
        kern,
        grid=(lay.batch, h),
        in_specs=[
            pl.BlockSpec((lay.p, dk), lambda b, i: (b, i)),
            pl.BlockSpec((lay.p, dk), lambda b, i: (b, h + i)),
            pl.BlockSpec((lay.p, dv), lambda b, i: (b, h + i)),
            pl.BlockSpec((lay.p, dv), lambda b, i: (b, 2 * h + i)),
            pl.BlockSpec((lay.p, LANES), lambda b, i: (b, 0)),
            pl.BlockSpec((LANES, dk), lambda b, i: (0, i)),
            pl.BlockSpec((1, dk), lambda b, i: (0, i)),
            pl.BlockSpec((LANES, dk), lambda b, i: (0, i)),
            pl.BlockSpec((1, dk), lambda b, i: (0, i)),
            pl.BlockSpec((1, dv), lambda b, i: (0, 0)),
        ],
        out_specs=pl.BlockSpec((lay.p, dv), lambda b, i: (b, i)),
        out_shape=jax.ShapeDtypeStruct((lay.t, d), BF16),
        scratch_shapes=[pltpu.VMEM((lay.p, dv), F32), pltpu.VMEM((lay.p, dv), F32),
                        pltpu.VMEM((dv, dk), F32), pltpu.VMEM((dv, dk), F32)],
        compiler_params=_params("arbitrary", "arbitrary"),
        name="gla_scan",
    )(qkvr, qkvr, qkvr, qkvr, a, waf, baf, wab, bab, norm_w)


def _route(logits, bias):
    n_rows = logits.shape[0]
    per_group = N_EXPERTS // N_GROUPS
    lane_i = lax.broadcasted_iota(jnp.int32, (n_rows, LANES), 1)
    lane = lane_i.astype(F32)
    group = (lane_i // per_group).astype(F32)
    valid = lane_i < N_EXPERTS
    big = float(2 * LANES)
    scores = jax.nn.sigmoid(logits)
    sel = jnp.where(valid, scores + bias, NEG_INF)

    def partner(x, s):
        return jnp.where((lane_i & s) == 0, pltpu.roll(x, LANES - s, axis=1), pltpu.roll(x, s, axis=1))

    def group_reduce(x, op):
        s = 1
        while s < per_group:
            x = op(x, partner(x, s))
            s *= 2
        return x

    m1 = group_reduce(sel, jnp.maximum)
    first = group_reduce(jnp.where(sel == m1, lane, big), jnp.minimum)
    m2 = group_reduce(jnp.where(lane == first, NEG_INF, sel), jnp.maximum)
    cur = jnp.where(valid, m1 + m2, NEG_INF)

    def pick_best(cur, ident):
        m = jnp.max(cur, axis=1, keepdims=True)
        best = jnp.min(jnp.where(cur == m, ident, big), axis=1, keepdims=True)
        return ident == best

    group_ok = jnp.zeros((n_rows, LANES), jnp.bool_)
    for _ in range(TOPK_GROUPS):
        p = pick_best(cur, group)
        group_ok = group_ok | p
        cur = jnp.where(p, NEG_INF, cur)

    cur = jnp.where(group_ok & valid, sel, NEG_INF)
    picked = jnp.zeros((n_rows, LANES), jnp.bool_)
    ids = jnp.zeros((n_rows, LANES), F32)
    for it in range(TOP_K):
        m = jnp.max(cur, axis=1, keepdims=True)
        best = jnp.min(jnp.where(cur == m, lane, big), axis=1, keepdims=True)
        p = lane == best
        picked = picked | p
        cur = jnp.where(p, NEG_INF, cur)
        ids = jnp.where(lane_i == it, best, ids)

    w = jnp.where(picked, scores, 0.0)
    gates = w / jnp.sum(w, axis=1, keepdims=True) * ROUTED_SCALE
    return gates, ids.astype(jnp.int32)


def _router_kernel(x_ref, nw_ref, shl, scl, shc, scc, rw_ref, rb_ref, hp_ref, g_ref, id_ref, h_ref, *, lay):
    _modnorm_rows(lay, x_ref, nw_ref, shl, scl, shc, scc, h_ref, packed_ref=hp_ref)
    logits = jnp.dot(h_ref[...], rw_ref[...], preferred_element_type=F32)
    g_ref[...], id_ref[...] = _route(logits, rb_ref[...])


def _router(lay, x, nw, mods, rw, rb):
    d = lay.d
    shl, shc = _mod_specs(lay, 3, two_axes=False)
    scl, scc = _mod_specs(lay, 4, two_axes=False)
    return pl.pallas_call(
        functools.partial(_router_kernel, lay=lay),
        grid=(lay.n_tiles,),
        in_specs=[
            pl.BlockSpec((lay.tm, d), lambda i: (i, 0)),
            pl.BlockSpec((1, d), lambda i: (0, 0)),
            shl, scl, shc, scc,
            pl.BlockSpec((d, LANES), lambda i: (0, 0)),
            pl.BlockSpec((1, LANES), lambda i: (0, 0)),
        ],
        out_specs=[pl.BlockSpec((lay.tm, d // 2), lambda i: (i, 0)),
                   pl.BlockSpec((lay.tm, LANES), lambda i: (i, 0)),
                   pl.BlockSpec((lay.tm, LANES), lambda i: (i, 0))],
        out_shape=[jax.ShapeDtypeStruct((lay.t, d // 2), U32),
                   jax.ShapeDtypeStruct((lay.t, LANES), F32),
                   jax.ShapeDtypeStruct((lay.t, LANES), jnp.int32)],
        scratch_shapes=[pltpu.VMEM((lay.tm, d), BF16)],
        compiler_params=_params("arbitrary"),
        name="moe_router",
    )(x, nw, mods, mods, mods, mods, rw, rb)


def _moe_plan(lay, gates, ids):
    t, rb = lay.t, MOE_ROWS
    n_pairs = t * TOP_K
    n_blocks = n_pairs // rb + N_EXPERTS
    assert N_EXPERTS * n_pairs < 2 ** 31
    eidx = ids[:, :TOP_K]
    w = jnp.take_along_axis(gates, eidx, axis=1).reshape(-1)
    key = eidx.reshape(-1) * n_pairs + jnp.arange(n_pairs, dtype=jnp.int32)
    key_s, w_s = lax.sort((key, w), num_keys=1)
    bounds = jnp.searchsorted(key_s, jnp.arange(N_EXPERTS + 1, dtype=jnp.int32) * n_pairs).astype(jnp.int32)
    start, counts = bounds[:-1], bounds[1:] - bounds[:-1]
    padded = (counts + rb - 1) // rb * rb
    pad_end = jnp.cumsum(padded)
    pad_start = pad_end - padded
    blk = jnp.arange(n_blocks + MOE_EXTRA_BLOCKS, dtype=jnp.int32) - 1
    e_blk = jnp.clip(jnp.searchsorted(pad_end, blk * rb, side="right", method="compare_all"),
                     0, N_EXPERTS - 1)
    r = jnp.arange(rb, dtype=jnp.int32)[None, :]
    in_run = (blk * rb - pad_start[e_blk])[:, None] + r
    valid = ((blk >= 0) & (blk * rb < pad_end[-1]))[:, None] & (in_run < counts[e_blk][:, None])
    src = jnp.clip(start[e_blk][:, None] + in_run, 0, n_pairs - 1)
    pair = key_s[src] % n_pairs
    tok = jnp.where(valid, pair // TOP_K, r)
    dst = jnp.where(valid, (pair % TOP_K) * t + pair // TOP_K, TOP_K * t + (blk % 2)[:, None] * rb + r)
    gate = jnp.where(valid, w_s[src], 0.0)[1:n_blocks + 1].reshape(n_blocks * rb, 1)
    n_used = pad_end[-1] // rb
    step_blk = jnp.minimum(jnp.arange(n_blocks + 1, dtype=jnp.int32), n_used - 1)
    block_expert = jnp.searchsorted(pad_end, step_blk * rb, side="right",
                                    method="compare_all").astype(jnp.int32)
    return (tok.astype(jnp.int32), dst.astype(jnp.int32), gate, block_expert,
            n_used.reshape(1).astype(jnp.int32))


def _moe_expert_kernel(be_ref, nu_ref, tok_hbm, dst_hbm, g_ref, win_ref, wout_ref, h_hbm, y_hbm,
                       hbuf, ybuf, winb, woutb, tok_smem, dst_smem, gsem, ssem, isem,
                       *, t_rows, f):
    nb = pl.program_id(0)
    pl.when(nb <= nu_ref[0])(functools.partial(
        _moe_expert_step, nb, nu_ref[0], be_ref, tok_hbm, dst_hbm, g_ref, win_ref, wout_ref, h_hbm,
        y_hbm, hbuf, ybuf, winb, woutb, tok_smem, dst_smem, gsem, ssem, isem, t_rows=t_rows, f=f))


def _moe_expert_step(nb, n_used, be_ref, tok_hbm, dst_hbm, g_ref, win_ref, wout_ref, h_hbm, y_hbm,
                     hbuf, ybuf, winb, woutb, tok_smem, dst_smem, gsem, ssem, isem, *, t_rows, f):
    rb = MOE_ROWS
    ys = nb % 2
    hs = nb % 3
    half = hbuf.shape[-1]
    dump0 = TOP_K * t_rows

    def ring(block):
        return pl.multiple_of(((block + 1) % MOE_ID_RING) * rb, rb)

    def ids_copies(block):
        k = ring(block)
        return (pltpu.make_async_copy(tok_hbm.at[block + 1], tok_smem.at[pl.ds(k, rb)], isem.at[0]),
                pltpu.make_async_copy(dst_hbm.at[block + 1], dst_smem.at[pl.ds(k, rb)], isem.at[1]))

    def gather_row(k, r, s):
        return pltpu.make_async_copy(h_hbm.at[pl.ds(tok_smem[k + r], 1)], hbuf.at[s, pl.ds(r, 1)],
                                     gsem.at[s])

    def scatter_row(k, r, s):
        return pltpu.make_async_copy(ybuf.at[s, pl.ds(r, 1)], y_hbm.at[pl.ds(dst_smem[k + r], 1)],
                                     ssem.at[s])

    def gather_wait(s):
        pltpu.make_async_copy(h_hbm.at[pl.ds(0, rb)], hbuf.at[s], gsem.at[s]).wait()

    def scatter_wait(s):
        pltpu.make_async_copy(ybuf.at[s], y_hbm.at[pl.ds(0, rb)], ssem.at[s]).wait()

    @pl.when(nb == 0)
    def _():
        for block in (-1, 0, 1, 2):
            for c in ids_copies(block):
                c.start()
                c.wait()
        ybuf[...] = jnp.zeros_like(ybuf)
        pltpu.make_async_copy(ybuf.at[0], y_hbm.at[pl.ds(dump0, rb)], ssem.at[0]).start()
        for block in (0, 1):
            def body(r, carry, block=block):
                gather_row((block + 1) * rb, r, block).start()
                return carry
            lax.fori_loop(0, rb, body, 0)

    @pl.when(nb > 0)
    def _():
        for c in ids_copies(nb + 2):
            c.wait()

    for c in ids_copies(nb + 3):
        c.start()

    @pl.when((nb == 0) | (be_ref[nb] != be_ref[jnp.maximum(nb - 1, 0)]))
    def _():
        winb[...] = win_ref[0].astype(BF16)
        woutb[...] = wout_ref[0].astype(BF16)

    gk = ring(nb + 2)
    sk = ring(nb - 1)
    ghs = (nb + 2) % 3
    n_groups = 8
    per_group = rb // n_groups

    def issue(group):
        rows = range(group * per_group, (group + 1) * per_group)
        copies = [gather_row(gk, r, ghs) for r in rows] + [scatter_row(sk, r, 1 - ys) for r in rows]
        for i, c in enumerate(copies):
            c.start(priority=i % 2)

    gather_wait(hs)
    issue(0)
    cols = half // 4
    hu = jnp.zeros((rb, 2 * f), F32)
    for piece in range(4):
        c0 = piece * cols
        lo, hi = _unpack_halves(hbuf[hs, :, c0:c0 + cols])
        hu = (hu + jnp.dot(lo.astype(BF16), winb[c0:c0 + cols, :], preferred_element_type=F32)
              + jnp.dot(hi.astype(BF16), winb[half + c0:half + c0 + cols, :],
                        preferred_element_type=F32))
        issue(1 + piece)
    act = (jax.nn.silu(hu[:, :f]) * hu[:, f:] * g_ref[...]).astype(BF16)
    scatter_wait(ys)
    for piece in range(4):
        c0 = piece * cols
        y_lo = jnp.dot(act, woutb[:, c0:c0 + cols], preferred_element_type=F32)
        y_hi = jnp.dot(act, woutb[:, half + c0:half + c0 + cols], preferred_element_type=F32)
        ybuf[ys, :, c0:c0 + cols] = _pack_halves(jnp.concatenate([y_lo, y_hi], axis=1))
        if piece < 3:
            issue(5 + piece)

    @pl.when(nb == n_used)
    def _():
        scatter_wait(1 - ys)
        gather_wait((nb + 1) % 3)
        gather_wait((nb + 2) % 3)
        for c in ids_copies(nb + 3):
            c.wait()


def _moe_experts(lay, hp, tok, dst, gate, block_expert, n_used, w_in, w_out):
    ne, d, f2 = w_in.shape
    f = f2 // 2
    rb = MOE_ROWS
    half = d // 2
    n_blocks = gate.shape[0] // rb
    assert tok.shape == dst.shape == (n_blocks + MOE_EXTRA_BLOCKS, rb) and rb <= lay.t
    kern = functools.partial(_moe_expert_kernel, t_rows=lay.t, f=f)
    last = n_blocks - 1
    grid_spec = pltpu.PrefetchScalarGridSpec(
        num_scalar_prefetch=2,
        grid=(n_blocks + 1,),
        in_specs=[
            pl.BlockSpec(memory_space=pl.ANY),
            pl.BlockSpec(memory_space=pl.ANY),
            pl.BlockSpec((rb, 1), lambda nb, be, nu: (jnp.minimum(nb, last), 0)),
            pl.BlockSpec((1, d, f2), lambda nb, be, nu: (be[nb], 0, 0)),
            pl.BlockSpec((1, f, d), lambda nb, be, nu: (be[nb], 0, 0)),
            pl.BlockSpec(memory_space=pl.ANY),
        ],
        out_specs=pl.BlockSpec(memory_space=pl.ANY),
        scratch_shapes=[
            pltpu.VMEM((3, rb, half), U32),
            pltpu.VMEM((2, rb, half), U32),
            pltpu.VMEM((d, f2), BF16),
            pltpu.VMEM((f, d), BF16),
            pltpu.SMEM((MOE_ID_RING * rb,), jnp.int32),
            pltpu.SMEM((MOE_ID_RING * rb,), jnp.int32),
            pltpu.SemaphoreType.DMA((3,)),
            pltpu.SemaphoreType.DMA((2,)),
            pltpu.SemaphoreType.DMA((2,)),
        ],
    )
    return pl.pallas_call(
        kern,
        grid_spec=grid_spec,
        out_shape=jax.ShapeDtypeStruct((TOP_K * lay.t + 2 * rb, half), U32),
        compiler_params=_params("arbitrary"),
        name="moe_experts",
    )(block_expert, n_used, tok, dst, gate, w_in, w_out, hp)


def _moe_combine_kernel(*refs, lay, f):
    y_refs = refs[:TOP_K]
    h_ref, wsi_ref, wso_ref, x_ref, gl_ref, gc_ref, o_ref = refs[TOP_K:]
    half = h_ref.shape[-1]
    lo, hi = _unpack_halves(y_refs[0][...])
    for y_ref in y_refs[1:]:
        l, h = _unpack_halves(y_ref[...])
        lo, hi = lo + l, hi + h
    hlo, hhi = _unpack_halves(h_ref[...])
    hu = (jnp.dot(hlo.astype(BF16), wsi_ref[:half, :], preferred_element_type=F32)
          + jnp.dot(hhi.astype(BF16), wsi_ref[half:, :], preferred_element_type=F32))
    act = (jax.nn.silu(hu[:, :f]) * hu[:, f:]).astype(BF16)
    shared = jnp.dot(act, wso_ref[...], preferred_element_type=F32)
    is_ctx = (pl.program_id(0) * COMBINE_ROWS) % lay.p < lay.ctx_len
    gate = jnp.where(is_ctx, gc_ref[0], gl_ref[0])
    o_ref[:, :half] = x_ref[:, :half] + gate[:, :half] * (lo + shared[:, :half])
    o_ref[:, half:] = x_ref[:, half:] + gate[:, half:] * (hi + shared[:, half:])


def _moe_combine(lay, y, hp, ws_in, ws_out, x, mods, gate_idx):
    d = lay.d
    half = d // 2
    rt = COMBINE_ROWS
    assert lay.ctx_len % rt == 0 and lay.p % rt == 0
    steps = lay.t // rt
    f = ws_out.shape[0]
    y_specs = [pl.BlockSpec((rt, half), functools.partial(lambda i, k: (k * steps + i, 0), k=k))
               for k in range(TOP_K)]
    return pl.pallas_call(
        functools.partial(_moe_combine_kernel, lay=lay, f=f),
        grid=(steps,),
        in_specs=y_specs + [
            pl.BlockSpec((rt, half), lambda i: (i, 0)),
            pl.BlockSpec((d, 2 * f), lambda i: (0, 0)),
            pl.BlockSpec((f, d), lambda i: (0, 0)),
            pl.BlockSpec((rt, d), lambda i: (i, 0)),
            pl.BlockSpec((1, 1, d), lambda i: (i * rt // lay.p, 0, gate_idx)),
            pl.BlockSpec((1, 1, d), lambda i: (lay.batch, 0, gate_idx)),
        ],
        out_specs=pl.BlockSpec((rt, d), lambda i: (i, 0)),
        out_shape=jax.ShapeDtypeStruct((lay.t, d), F32),
        input_output_aliases={TOP_K + 3: 0},
        compiler_params=_params("arbitrary"),
        name="moe_combine",
    )(*([y] * TOP_K), hp, ws_in, ws_out, x, mods, mods)


def _final_norm_kernel(x_ref, w_ref, o_ref):
    x = x_ref[0]
    o_ref[0] = x * lax.rsqrt(jnp.mean(x * x, axis=-1, keepdims=True) + EPS) * w_ref[...]


def _final_norm(lay, x, w):
    d = lay.d
    rows = _pick(lay.seq, (256, 128, 64))
    assert lay.ctx_len % rows == 0
    off = lay.ctx_len // rows
    return pl.pallas_call(
        _final_norm_kernel,
        grid=(lay.batch, lay.seq // rows),
        in_specs=[pl.BlockSpec((1, rows, d), lambda b, j: (b, off + j, 0)),
                  pl.BlockSpec((1, d), lambda b, j: (0, 0))],
        out_specs=pl.BlockSpec((1, rows, d), lambda b, j: (b, j, 0)),
        out_shape=jax.ShapeDtypeStruct((lay.batch, lay.seq, d), F32),
        compiler_params=_params("arbitrary", "arbitrary"),
        name="final_norm",
    )(x.reshape(lay.batch, lay.p, d), w)


def kernel(x, c, ctx, c_ctx, ada_w_down, ada_w_up, ada_b, norm_mix, norm_ffn, conv_w_in, conv_w, conv_w_out, gla_w_in, gla_w_a2_fwd, gla_b_a2_fwd, gla_w_a2_bwd, gla_b_a2_bwd, gla_norm, gla_w_out, router_w, router_bias, exp_w_in, exp_w_out, shared_w_in, shared_w_out, norm_final):
    batch, seq, d = x.shape
    ctx_len = ctx.shape[1]
    depth = ada_w_down.shape[0]
    lay = _Layout(batch, ctx_len, seq, d)
    qk = d // 2
    tn = _pick(d, (512, 256, 128))

    cond = jnp.zeros((MOD_ROWS, d), F32).at[:batch].set(c).at[batch].set(c_ctx)
    mods_all = _ada_all(cond, ada_w_down, ada_w_up, ada_b)
    xs = jnp.concatenate([ctx, x], axis=1).reshape(lay.t, d)

    for i in range(depth):
        j = i // 2
        mods = mods_all[i].reshape(MOD_ROWS, 1, N_MOD * d)
        nw_mix = norm_mix[i].reshape(1, d)
        if i % 2 == 0:
            y = _norm_proj(lay, xs, nw_mix, mods, 0, 1, conv_w_in[j].astype(BF16), out_dtype=BF16,
                           tn=_pick(d, (256, 128)), conv_w=conv_w[j], name="conv_in")
            xs = _out_proj(lay, y, conv_w_out[j].astype(BF16), xs, mods, 2, tn=tn, name="conv_out")
        else:
            n_main = 2 * qk + 2 * d
            w_aux = jnp.zeros((d, LANES), BF16).at[:, :2 * GLA_GATE_RANK].set(
                gla_w_in[j][:, n_main:].astype(BF16))
            qkvr, a = _norm_proj(lay, xs, nw_mix, mods, 0, 1, gla_w_in[j].astype(BF16), out_dtype=BF16,
                                 tn=tn, n=n_main, w_aux=w_aux, name="gla_in")
            waf = jnp.zeros((LANES, qk), BF16).at[:GLA_GATE_RANK].set(gla_w_a2_fwd[j].astype(BF16))
            wab = jnp.zeros((LANES, qk), BF16).at[GLA_GATE_RANK:2 * GLA_GATE_RANK].set(
                gla_w_a2_bwd[j].astype(BF16))
            y = _gla(lay, qkvr, a, waf, gla_b_a2_fwd[j].reshape(1, qk), wab,
                     gla_b_a2_bwd[j].reshape(1, qk), gla_norm[j].reshape(1, d // GLA_HEADS))
            xs = _out_proj(lay, y, gla_w_out[j].astype(BF16), xs, mods, 2, tn=tn, name="gla_out")

        rw = jnp.zeros((d, LANES), BF16).at[:, :N_EXPERTS].set(router_w[i].astype(BF16))
        rb = jnp.zeros((1, LANES), F32).at[0, :N_EXPERTS].set(router_bias[i])
        hp, gates, ids = _router(lay, xs, norm_ffn[i].reshape(1, d), mods, rw, rb)
        tok, dst, gate, block_expert, n_used = _moe_plan(lay, gates, ids)
        y = _moe_experts(lay, hp, tok, dst, gate, block_expert, n_used, exp_w_in[i], exp_w_out[i])
        xs = _moe_combine(lay, y, hp, shared_w_in[i].astype(BF16), shared_w_out[i].astype(BF16),
                          xs, mods, 5)

    return _final_norm(lay, xs, norm_final.reshape(1, d))
```

```python
import functools

import jax
import jax.numpy as jnp
import numpy as np
from jax import lax
from jax.experimental import pallas as pl
from jax.experimental.pallas import tpu as pltpu

GRID_W = 64
EPS = 1e-6
N_MOD = 6
GLA_HEADS = 8
GLA_GATE_RANK = 16
GLA_GATE_TAU = 16.0
GLA_LOG_DECAY_FLOOR = -1.0
N_EXPERTS = 64
N_GROUPS = 8
TOPK_GROUPS = 4
TOP_K = 8
ROUTED_SCALE = 2.5

LANES = 128
GLA_ROWS = 128
TILES_PER_SAMPLE = 4
MOD_ROWS = 16
VMEM_LIMIT_BYTES = 56 * 1024 * 1024

MOE_ROWS = 256
COMBINE_ROWS = 128
MOE_EXTRA_BLOCKS = 5
MOE_ID_RING = 8
COMBINE_POS_RING = 4

BF16 = jnp.bfloat16
F32 = jnp.float32
U32 = jnp.uint32
HIGH_HALF = np.uint32(0xFFFF0000)
NEG_INF = float("-inf")


def _params(*sem):
    return pltpu.CompilerParams(dimension_semantics=sem, vmem_limit_bytes=VMEM_LIMIT_BYTES)


def _pick(n, prefs):
    for p in prefs:
        if n % p == 0:
            return p
    return n


def _ada_kernel(cond_ref, wd_ref, wu_ref, b_ref, o_ref, z_ref):
    @pl.when(pl.program_id(1) == 0)
    def _():
        s = jax.nn.silu(cond_ref[...]).astype(BF16)
        z_ref[...] = jnp.dot(s, wd_ref[0].astype(BF16), preferred_element_type=F32)

    o_ref[0] = jnp.dot(z_ref[...].astype(BF16), wu_ref[0].astype(BF16),
                       preferred_element_type=F32) + b_ref[0]


def _ada_all(cond, w_down, w_up, b):
    depth, d, r = w_down.shape
    n = w_up.shape[-1]
    tn = _pick(n, (2048, 1024, 512, 256, 128))
    return pl.pallas_call(
        _ada_kernel,
        grid=(depth, n // tn),
        in_specs=[
            pl.BlockSpec((MOD_ROWS, d), lambda l, j: (0, 0)),
            pl.BlockSpec((1, d, r), lambda l, j: (l, 0, 0)),
            pl.BlockSpec((1, r, tn), lambda l, j: (l, 0, j)),
            pl.BlockSpec((1, 1, tn), lambda l, j: (l, 0, j)),
        ],
        out_specs=pl.BlockSpec((1, MOD_ROWS, tn), lambda l, j: (l, 0, j)),
        out_shape=jax.ShapeDtypeStruct((depth, MOD_ROWS, n), F32),
        scratch_shapes=[pltpu.VMEM((MOD_ROWS, r), F32)],
        compiler_params=_params("arbitrary", "arbitrary"),
        name="ada_mod",
    )(cond, w_down, w_up, b.reshape(depth, 1, n))


class _Layout:
    def __init__(self, batch, ctx_len, seq, d):
        self.batch, self.ctx_len, self.seq, self.d = batch, ctx_len, seq, d
        self.p = ctx_len + seq
        self.t = batch * self.p
        assert self.p % TILES_PER_SAMPLE == 0
        self.tm = self.p // TILES_PER_SAMPLE
        assert GRID_W & (GRID_W - 1) == 0
        assert self.tm % GRID_W == 0 and ctx_len % GRID_W == 0 and self.tm % 16 == 0
        assert ctx_len <= self.tm and batch < MOD_ROWS
        self.n_tiles = self.t // self.tm
        self.slab = _pick(self.tm, (32, 16))
        assert ctx_len % self.slab == 0


def _mod_specs(lay, idx, two_axes=True):
    d = lay.d
    if two_axes:
        lat = pl.BlockSpec((1, 1, d), lambda i, j: (i // TILES_PER_SAMPLE, 0, idx))
        ctx = pl.BlockSpec((1, 1, d), lambda i, j: (lay.batch, 0, idx))
    else:
        lat = pl.BlockSpec((1, 1, d), lambda i: (i // TILES_PER_SAMPLE, 0, idx))
        ctx = pl.BlockSpec((1, 1, d), lambda i: (lay.batch, 0, idx))
    return lat, ctx


def _tile_row0(lay):
    return (pl.program_id(0) % TILES_PER_SAMPLE) * lay.tm


def _row_in_sample(lay, shape):
    return _tile_row0(lay) + lax.broadcasted_iota(jnp.int32, shape, 0)


def _pack_halves(x):
    half = x.shape[1] // 2
    xb = x.astype(BF16).astype(F32)
    lo = lax.bitcast_convert_type(xb[:, :half], U32) >> 16
    hi = lax.bitcast_convert_type(xb[:, half:], U32) & HIGH_HALF
    return hi | lo


def _unpack_halves(w):
    lo = lax.bitcast_convert_type(w << 16, F32)
    hi = lax.bitcast_convert_type(w & HIGH_HALF, F32)
    return lo, hi


def _modnorm_rows(lay, x_ref, nw_ref, shl_ref, scl_ref, shc_ref, scc_ref, h_ref, packed_ref=None):
    nw = nw_ref[...]
    slab = lay.slab
    row0 = _tile_row0(lay)

    def body(s, carry):
        r = pl.multiple_of(s * slab, slab)
        x = x_ref[pl.ds(r, slab), :]
        ms = jnp.mean(x * x, axis=-1, keepdims=True)
        y = x * lax.rsqrt(ms + EPS) * nw
        is_ctx = row0 + r < lay.ctx_len
        sc = jnp.where(is_ctx, scc_ref[0], scl_ref[0])
        sh = jnp.where(is_ctx, shc_ref[0], shl_ref[0])
        h = y * (1.0 + sc) + sh
        h_ref[pl.ds(r, slab), :] = h.astype(h_ref.dtype)
        if packed_ref is not None:
            packed_ref[pl.ds(r, slab), :] = _pack_halves(h)
        return carry

    lax.fori_loop(0, lay.tm // slab, body, 0)


def _proj_kernel(x_ref, nw_ref, shl, scl, shc, scc, w_ref, o_ref, h_ref, *, lay):
    @pl.when(pl.program_id(1) == 0)
    def _():
        _modnorm_rows(lay, x_ref, nw_ref, shl, scl, shc, scc, h_ref)

    o_ref[...] = jnp.dot(h_ref[...], w_ref[...], preferred_element_type=F32).astype(o_ref.dtype)


def _proj_aux_kernel(x_ref, nw_ref, shl, scl, shc, scc, w_ref, wa_ref, o_ref, a_ref, h_ref, *, lay):
    @pl.when(pl.program_id(1) == 0)
    def _():
        _modnorm_rows(lay, x_ref, nw_ref, shl, scl, shc, scc, h_ref)
        a_ref[...] = jnp.dot(h_ref[...], wa_ref[...], preferred_element_type=F32)

    o_ref[...] = jnp.dot(h_ref[...], w_ref[...], preferred_element_type=F32).astype(o_ref.dtype)


def _conv_proj_kernel(x_ref, nw_ref, shl, scl, shc, scc, wb_ref, wc_ref, wv_ref, cw_ref, o_ref, h_ref,
                      *, lay, tc):
    @pl.when(pl.program_id(1) == 0)
    def _():
        _modnorm_rows(lay, x_ref, nw_ref, shl, scl, shc, scc, h_ref)

    tm = lay.tm
    h = h_ref[...]
    bg = jnp.dot(h, wb_ref[...], preferred_element_type=F32)
    cg = jnp.dot(h, wc_ref[...], preferred_element_type=F32)
    v = jnp.dot(h, wv_ref[...], preferred_element_type=F32)
    u = cg * v
    row = _row_in_sample(lay, (tm, tc))
    is_ctx = row < lay.ctx_len
    col = (row - lay.ctx_len) & (GRID_W - 1)
    first = (is_ctx & (row == 0)) | (~is_ctx & (col == 0))
    last = (is_ctx & (row == lay.ctx_len - 1)) | (~is_ctx & (col == GRID_W - 1))
    prev = jnp.where(first, 0.0, pltpu.roll(u, 1, axis=0))
    nxt = jnp.where(last, 0.0, pltpu.roll(u, tm - 1, axis=0))
    cw = cw_ref[...]
    conv = prev * cw[0:1, :] + u * cw[1:2, :] + nxt * cw[2:3, :]
    o_ref[...] = (bg * conv).astype(o_ref.dtype)


def _norm_proj(lay, x, nw, mods, shift_idx, scale_idx, w, *, out_dtype, tn, n=None, w_aux=None,
               conv_w=None, name):
    d = w.shape[0]
    n = w.shape[1] if n is None else n
    assert n % tn == 0
    shl, shc = _mod_specs(lay, shift_idx)
    scl, scc = _mod_specs(lay, scale_idx)
    in_specs = [
        pl.BlockSpec((lay.tm, d), lambda i, j: (i, 0)),
        pl.BlockSpec((1, d), lambda i, j: (0, 0)),
        shl, scl, shc, scc,
        pl.BlockSpec((d, tn), lambda i, j: (0, j)),
    ]
    args = [x, nw, mods, mods, mods, mods, w]
    scratch = [pltpu.VMEM((lay.tm, d), BF16)]
    params = _params("arbitrary", "arbitrary")
    if conv_w is not None:
        tc = tn
        nct = n // 3 // tc
        in_specs[-1:] = [pl.BlockSpec((d, tc), functools.partial(lambda i, j, k: (0, k * nct + j), k=k))
                         for k in range(3)]
        in_specs.append(pl.BlockSpec((3, tc), lambda i, j: (0, j)))
        return pl.pallas_call(
            functools.partial(_conv_proj_kernel, lay=lay, tc=tc),
            grid=(lay.n_tiles, nct), in_specs=in_specs,
            out_specs=pl.BlockSpec((lay.tm, tc), lambda i, j: (i, j)),
            out_shape=jax.ShapeDtypeStruct((lay.t, n // 3), out_dtype),
            scratch_shapes=scratch, compiler_params=params, name=name,
        )(*args[:-1], w, w, w, conv_w)
    if w_aux is not None:
        na = w_aux.shape[1]
        in_specs.append(pl.BlockSpec((d, na), lambda i, j: (0, 0)))
        return pl.pallas_call(
            functools.partial(_proj_aux_kernel, lay=lay),
            grid=(lay.n_tiles, n // tn), in_specs=in_specs,
            out_specs=[pl.BlockSpec((lay.tm, tn), lambda i, j: (i, j)),
                       pl.BlockSpec((lay.tm, na), lambda i, j: (i, 0))],
            out_shape=[jax.ShapeDtypeStruct((lay.t, n), out_dtype),
                       jax.ShapeDtypeStruct((lay.t, na), F32)],
            scratch_shapes=scratch, compiler_params=params, name=name,
        )(*args, w_aux)
    return pl.pallas_call(
        functools.partial(_proj_kernel, lay=lay),
        grid=(lay.n_tiles, n // tn), in_specs=in_specs,
        out_specs=pl.BlockSpec((lay.tm, tn), lambda i, j: (i, j)),
        out_shape=jax.ShapeDtypeStruct((lay.t, n), out_dtype),
        scratch_shapes=scratch, compiler_params=params, name=name,
    )(*args)


def _out_proj_kernel(a_ref, w_ref, x_ref, gl_ref, gc_ref, o_ref, *, lay):
    y = jnp.dot(a_ref[...], w_ref[...], preferred_element_type=F32)
    is_ctx = _row_in_sample(lay, y.shape) < lay.ctx_len
    gate = jnp.where(is_ctx, gc_ref[0], gl_ref[0])
    o_ref[...] = x_ref[...] + gate * y


def _out_proj(lay, a, w, x, mods, gate_idx, *, tn, name):
    k, d = w.shape
    gl = pl.BlockSpec((1, 1, tn), lambda i, j: (i // TILES_PER_SAMPLE, 0, gate_idx * (d // tn) + j))
    gc = pl.BlockSpec((1, 1, tn), lambda i, j: (lay.batch, 0, gate_idx * (d // tn) + j))
    return pl.pallas_call(
        functools.partial(_out_proj_kernel, lay=lay),
        grid=(lay.n_tiles, d // tn),
        in_specs=[
            pl.BlockSpec((lay.tm, k), lambda i, j: (i, 0)),
            pl.BlockSpec((k, tn), lambda i, j: (0, j)),
            pl.BlockSpec((lay.tm, tn), lambda i, j: (i, j)),
            gl, gc,
        ],
        out_specs=pl.BlockSpec((lay.tm, tn), lambda i, j: (i, j)),
        out_shape=jax.ShapeDtypeStruct((lay.t, d), F32),
        input_output_aliases={2: 0},
        compiler_params=_params("arbitrary", "arbitrary"),
        name=name,
    )(a, w, x, mods, mods)


def _gla_kernel(q_ref, k_ref, v_ref, r_ref, a_ref, waf_ref, baf_ref, wab_ref, bab_ref, nw_ref,
                y_ref, of_ref, ob_ref, sf_ref, sb_ref, *, n_chunks, n_ctx_chunks, dk):
    c_rows = GLA_ROWS
    scale = dk ** -0.5
    ri = lax.broadcasted_iota(jnp.int32, (c_rows, c_rows), 0)
    ci = lax.broadcasted_iota(jnp.int32, (c_rows, c_rows), 1)
    nt = (((1,), (1,)), ((), ()))
    tn_dims = (((0,), (0,)), ((), ()))

    def chunk(c, forward, s_ref):
        rows = pl.ds(pl.multiple_of(c * c_rows, c_rows), c_rows)
        keep = (ci <= ri) if forward else (ci >= ri)
        tri = keep.astype(BF16)
        wa, ba = (waf_ref, baf_ref) if forward else (wab_ref, bab_ref)
        z = jnp.dot(a_ref[rows, :].astype(BF16), wa[...], preferred_element_type=F32) + ba[...]
        g = jnp.maximum(jax.nn.log_sigmoid(z) / GLA_GATE_TAU, GLA_LOG_DECAY_FLOOR)
        g1 = g.astype(BF16)
        rem = g - g1.astype(F32)
        g2 = rem.astype(BF16)
        g3 = (rem - g2.astype(F32)).astype(BF16)
        bcum = (jnp.dot(tri, g1, preferred_element_type=F32)
                + jnp.dot(tri, g2, preferred_element_type=F32)
                + jnp.dot(tri, g3, preferred_element_type=F32))
        mid = bcum[c_rows // 2:c_rows // 2 + 1, :]
        tot = bcum[c_rows - 1:c_rows, :] if forward else bcum[0:1, :]
        q = q_ref[rows, :].astype(F32) * scale
        k = k_ref[rows, :].astype(F32)
        v = v_ref[rows, :]
        q_rel = (q * jnp.exp(bcum - mid)).astype(BF16)
        k_rel = (k * jnp.exp(mid - bcum)).astype(BF16)
        q_dec = (q * jnp.exp(bcum)).astype(BF16)
        k_end = (k * jnp.exp(tot - bcum)).astype(BF16)
        att = lax.dot_general(q_rel, k_rel, nt, preferred_element_type=F32)
        att = jnp.where(keep, att, 0.0).astype(BF16)
        s = s_ref[...]
        o = (jnp.dot(att, v, preferred_element_type=F32)
             + lax.dot_general(q_dec, s.astype(BF16), nt, preferred_element_type=F32))
        s_ref[...] = jnp.exp(tot) * s + lax.dot_general(v, k_end, tn_dims, preferred_element_type=F32)
        return rows, o

    sf_ref[...] = jnp.zeros_like(sf_ref)
    sb_ref[...] = jnp.zeros_like(sb_ref)

    def scan_body(n, carry):
        rows, o = chunk(n, True, sf_ref)
        of_ref[rows, :] = o
        c = jnp.where(n < n_ctx_chunks, n_ctx_chunks - 1 - n, n_chunks - 1 - (n - n_ctx_chunks))
        rows, o = chunk(c, False, sb_ref)
        ob_ref[rows, :] = o
        return carry

    lax.fori_loop(0, n_chunks, scan_body, 0)
    nw = nw_ref[...]

    def out_body(n, carry):
        rows = pl.ds(pl.multiple_of(n * c_rows, c_rows), c_rows)
        o = of_ref[rows, :] + ob_ref[rows, :]
        o = o * lax.rsqrt(jnp.mean(o * o, axis=-1, keepdims=True) + EPS) * nw
        y_ref[rows, :] = (o * jax.nn.silu(r_ref[rows, :].astype(F32))).astype(y_ref.dtype)
        return carry

    lax.fori_loop(0, n_chunks, out_body, 0)


def _gla(lay, qkvr, a, waf, baf, wab, bab, norm_w):
    d = lay.d
    dk = d // 2 // GLA_HEADS
    dv = d // GLA_HEADS
    assert lay.ctx_len % GLA_ROWS == 0 and lay.seq % GLA_ROWS == 0
    n_chunks = lay.p // GLA_ROWS
    kern = functools.partial(_gla_kernel, n_chunks=n_chunks,
                             n_ctx_chunks=lay.ctx_len // GLA_ROWS, dk=dk)
    h = GLA_HEADS
    return pl.pallas_call(
        kern,
        grid=(lay.batch, h),
        in_specs=[
            pl.BlockSpec((lay.p, dk), lambda b, i: (b, i)),
            pl.BlockSpec((lay.p, dk), lambda b, i: (b, h + i)),
            pl.BlockSpec((lay.p, dv), lambda b, i: (b, h + i)),
            pl.BlockSpec((lay.p, dv), lambda b, i: (b, 2 * h + i)),
            pl.BlockSpec((lay.p, LANES), lambda b, i: (b, 0)),
            pl.BlockSpec((LANES, dk), lambda b, i: (0, i)),
            pl.BlockSpec((1, dk), lambda b, i: (0, i)),
            pl.BlockSpec((LANES, dk), lambda b, i: (0, i)),
            pl.BlockSpec((1, dk), lambda b, i: (0, i)),
            pl.BlockSpec((1, dv), lambda b, i: (0, 0)),
        ],
        out_specs=pl.BlockSpec((lay.p, dv), lambda b, i: (b, i)),
        out_shape=jax.ShapeDtypeStruct((lay.t, d), BF16),
        scratch_shapes=[pltpu.VMEM((lay.p, dv), F32), pltpu.VMEM((lay.p, dv), F32),
                        pltpu.VMEM((dv, dk), F32), pltpu.VMEM((dv, dk), F32)],
        compiler_params=_params("arbitrary", "arbitrary"),
        name="gla_scan",
    )(qkvr, qkvr, qkvr, qkvr, a, waf, baf, wab, bab, norm_w)


def _route(logits, bias):
    n_rows = logits.shape[0]
    per_group = N_EXPERTS // N_GROUPS
    lane_i = lax.broadcasted_iota(jnp.int32, (n_rows, LANES), 1)
    lane = lane_i.astype(F32)
    group = (lane_i // per_group).astype(F32)
    valid = lane_i < N_EXPERTS
    big = float(2 * LANES)
    scores = jax.nn.sigmoid(logits)
    sel = jnp.where(valid, scores + bias, NEG_INF)

    def partner(x, s):
        return jnp.where((lane_i & s) == 0, pltpu.roll(x, LANES - s, axis=1), pltpu.roll(x, s, axis=1))

    def group_reduce(x, op):
        s = 1
        while s < per_group:
            x = op(x, partner(x, s))
            s *= 2
        return x

    m1 = group_reduce(sel, jnp.maximum)
    first = group_reduce(jnp.where(sel == m1, lane, big), jnp.minimum)
    m2 = group_reduce(jnp.where(lane == first, NEG_INF, sel), jnp.maximum)
    cur = jnp.where(valid, m1 + m2, NEG_INF)

    def pick_best(cur, ident):
        m = jnp.max(cur, axis=1, keepdims=True)
        best = jnp.min(jnp.where(cur == m, ident, big), axis=1, keepdims=True)
        return ident == best

    group_ok = jnp.zeros((n_rows, LANES), jnp.bool_)
    for _ in range(TOPK_GROUPS):
        p = pick_best(cur, group)
        group_ok = group_ok | p
        cur = jnp.where(p, NEG_INF, cur)

    cur = jnp.where(group_ok & valid, sel, NEG_INF)
    picked = jnp.zeros((n_rows, LANES), jnp.bool_)
    ids = jnp.zeros((n_rows, LANES), F32)
    for it in range(TOP_K):
        m = jnp.max(cur, axis=1, keepdims=True)
        best = jnp.min(jnp.where(cur == m, lane, big), axis=1, keepdims=True)
        p = lane == best
        picked = picked | p
        cur = jnp.where(p, NEG_INF, cur)
        ids = jnp.where(lane_i == it, best, ids)

    w = jnp.where(picked, scores, 0.0)
    gates = w / jnp.sum(w, axis=1, keepdims=True) * ROUTED_SCALE
    return gates, ids.astype(jnp.int32)


def _router_kernel(x_ref, nw_ref, shl, scl, shc, scc, rw_ref, rb_ref, hp_ref, g_ref, id_ref, h_ref, *, lay):
    _modnorm_rows(lay, x_ref, nw_ref, shl, scl, shc, scc, h_ref, packed_ref=hp_ref)
    logits = jnp.dot(h_ref[...], rw_ref[...], preferred_element_type=F32)
    g_ref[...], id_ref[...] = _route(logits, rb_ref[...])


def _router(lay, x, nw, mods, rw, rb):
    d = lay.d
    shl, shc = _mod_specs(lay, 3, two_axes=False)
    scl, scc = _mod_specs(lay, 4, two_axes=False)
    return pl.pallas_call(
        functools.partial(_router_kernel, lay=lay),
        grid=(lay.n_tiles,),
        in_specs=[
            pl.BlockSpec((lay.tm, d), lambda i: (i, 0)),
            pl.BlockSpec((1, d), lambda i: (0, 0)),
            shl, scl, shc, scc,
            pl.BlockSpec((d, LANES), lambda i: (0, 0)),
            pl.BlockSpec((1, LANES), lambda i: (0, 0)),
        ],
        out_specs=[pl.BlockSpec((lay.tm, d // 2), lambda i: (i, 0)),
                   pl.BlockSpec((lay.tm, LANES), lambda i: (i, 0)),
                   pl.BlockSpec((lay.tm, LANES), lambda i: (i, 0))],
        out_shape=[jax.ShapeDtypeStruct((lay.t, d // 2), U32),
                   jax.ShapeDtypeStruct((lay.t, LANES), F32),
                   jax.ShapeDtypeStruct((lay.t, LANES), jnp.int32)],
        scratch_shapes=[pltpu.VMEM((lay.tm, d), BF16)],
        compiler_params=_params("arbitrary"),
        name="moe_router",
    )(x, nw, mods, mods, mods, mods, rw, rb)


def _moe_plan(lay, gates, ids):
    t, rb = lay.t, MOE_ROWS
    n_pairs = t * TOP_K
    n_blocks = n_pairs // rb + N_EXPERTS
    assert N_EXPERTS * n_pairs < 2 ** 31
    eidx = ids[:, :TOP_K]
    w = jnp.take_along_axis(gates, eidx, axis=1).reshape(-1)
    key = eidx.reshape(-1) * n_pairs + jnp.arange(n_pairs, dtype=jnp.int32)
    key_s, w_s = lax.sort((key, w), num_keys=1)
    bounds = jnp.searchsorted(key_s, jnp.arange(N_EXPERTS + 1, dtype=jnp.int32) * n_pairs).astype(jnp.int32)
    start, counts = bounds[:-1], bounds[1:] - bounds[:-1]
    padded = (counts + rb - 1) // rb * rb
    pad_end = jnp.cumsum(padded)
    pad_start = pad_end - padded
    blk = jnp.arange(n_blocks + MOE_EXTRA_BLOCKS, dtype=jnp.int32) - 1
    e_blk = jnp.clip(jnp.searchsorted(pad_end, blk * rb, side="right", method="compare_all"),
                     0, N_EXPERTS - 1)
    r = jnp.arange(rb, dtype=jnp.int32)[None, :]
    in_run = (blk * rb - pad_start[e_blk])[:, None] + r
    valid = ((blk >= 0) & (blk * rb < pad_end[-1]))[:, None] & (in_run < counts[e_blk][:, None])
    src = jnp.clip(start[e_blk][:, None] + in_run, 0, n_pairs - 1)
    pair = key_s[src] % n_pairs
    tok = jnp.where(valid, pair // TOP_K, r)
    gate = jnp.where(valid, w_s[src], 0.0)[1:n_blocks + 1].reshape(n_blocks * rb, 1)
    n_used = pad_end[-1] // rb
    step_blk = jnp.minimum(jnp.arange(n_blocks + 1, dtype=jnp.int32), n_used - 1)
    block_expert = jnp.searchsorted(pad_end, step_blk * rb, side="right",
                                    method="compare_all").astype(jnp.int32)
    chose = jnp.sum(eidx[:, :, None] == jnp.arange(N_EXPERTS, dtype=jnp.int32)[None, None, :], axis=1,
                    dtype=jnp.int32)
    earlier = jnp.cumsum(chose, axis=0) - chose
    pos = pad_start[eidx] + jnp.take_along_axis(earlier, eidx, axis=1)
    rt = COMBINE_ROWS
    pos = pos.reshape(t // rt, rt, TOP_K).transpose(0, 2, 1).reshape(t // rt, TOP_K * rt)
    pos = jnp.concatenate([pos, jnp.zeros((2, TOP_K * rt), pos.dtype)]).astype(jnp.int32)
    return tok.astype(jnp.int32), gate, block_expert, n_used.reshape(1).astype(jnp.int32), pos


def _moe_expert_kernel(be_ref, nu_ref, tok_hbm, g_ref, win_ref, wout_ref, h_hbm, y_ref,
                       hbuf, winb, woutb, tok_smem, gsem, isem, *, f):
    nb = pl.program_id(0)
    pl.when(nb <= nu_ref[0])(functools.partial(
        _moe_expert_step, nb, nu_ref[0], be_ref, tok_hbm, g_ref, win_ref, wout_ref, h_hbm, y_ref,
        hbuf, winb, woutb, tok_smem, gsem, isem, f=f))


def _moe_expert_step(nb, n_used, be_ref, tok_hbm, g_ref, win_ref, wout_ref, h_hbm, y_ref,
                     hbuf, winb, woutb, tok_smem, gsem, isem, *, f):
    rb = MOE_ROWS
    hs = nb % 3
    half = hbuf.shape[-1]

    def ring(block):
        return pl.multiple_of(((block + 1) % MOE_ID_RING) * rb, rb)

    def ids_copy(block):
        return pltpu.make_async_copy(tok_hbm.at[block + 1], tok_smem.at[pl.ds(ring(block), rb)], isem)

    def gather_row(k, r, s):
        return pltpu.make_async_copy(h_hbm.at[pl.ds(tok_smem[k + r], 1)], hbuf.at[s, pl.ds(r, 1)],
                                     gsem.at[s])

    def gather_wait(s):
        pltpu.make_async_copy(h_hbm.at[pl.ds(0, rb)], hbuf.at[s], gsem.at[s]).wait()

    @pl.when(nb == 0)
    def _():
        for block in (0, 1, 2):
            c = ids_copy(block)
            c.start()
            c.wait()
        for block in (0, 1):
            def body(r, carry, block=block):
                gather_row((block + 1) * rb, r, block).start()
                return carry
            lax.fori_loop(0, rb, body, 0)

    @pl.when(nb > 0)
    def _():
        ids_copy(nb + 2).wait()

    ids_copy(nb + 3).start()

    @pl.when((nb == 0) | (be_ref[nb] != be_ref[jnp.maximum(nb - 1, 0)]))
    def _():
        winb[...] = win_ref[0, 0].astype(BF16)
        woutb[...] = wout_ref[0, 0].astype(BF16)

    gk = ring(nb + 2)
    ghs = (nb + 2) % 3
    n_groups = 8
    per_group = rb // n_groups

    def issue(group):
        for r in range(group * per_group, (group + 1) * per_group):
            gather_row(gk, r, ghs).start(priority=r % 2)

    gather_wait(hs)
    issue(0)
    cols = half // 4
    hu = jnp.zeros((rb, 2 * f), F32)
    for piece in range(4):
        c0 = piece * cols
        lo, hi = _unpack_halves(hbuf[hs, :, c0:c0 + cols])
        hu = (hu + jnp.dot(lo.astype(BF16), winb[c0:c0 + cols, :], preferred_element_type=F32)
              + jnp.dot(hi.astype(BF16), winb[half + c0:half + c0 + cols, :],
                        preferred_element_type=F32))
        issue(1 + piece)
    act = (jax.nn.silu(hu[:, :f]) * hu[:, f:] * g_ref[...]).astype(BF16)
    for piece in range(4):
        c0 = piece * cols
        y_lo = jnp.dot(act, woutb[:, c0:c0 + cols], preferred_element_type=F32)
        y_hi = jnp.dot(act, woutb[:, half + c0:half + c0 + cols], preferred_element_type=F32)
        y_ref[:, c0:c0 + cols] = _pack_halves(jnp.concatenate([y_lo, y_hi], axis=1))
        if piece < 3:
            issue(5 + piece)

    @pl.when(nb == n_used)
    def _():
        gather_wait((nb + 1) % 3)
        gather_wait((nb + 2) % 3)
        ids_copy(nb + 3).wait()


def _moe_experts(lay, layer, hp, tok, gate, block_expert, n_used, w_in, w_out):
    _, ne, d, f2 = w_in.shape
    f = f2 // 2
    rb = MOE_ROWS
    half = d // 2
    n_blocks = gate.shape[0] // rb
    assert tok.shape == (n_blocks + MOE_EXTRA_BLOCKS, rb) and rb <= lay.t
    kern = functools.partial(_moe_expert_kernel, f=f)
    last = n_blocks - 1
    grid_spec = pltpu.PrefetchScalarGridSpec(
        num_scalar_prefetch=2,
        grid=(n_blocks + 1,),
        in_specs=[
            pl.BlockSpec(memory_space=pl.ANY),
            pl.BlockSpec((rb, 1), lambda nb, be, nu: (jnp.minimum(nb, last), 0)),
            pl.BlockSpec((1, 1, d, f2), lambda nb, be, nu: (layer, be[nb], 0, 0)),
            pl.BlockSpec((1, 1, f, d), lambda nb, be, nu: (layer, be[nb], 0, 0)),
            pl.BlockSpec(memory_space=pl.ANY),
        ],
        out_specs=pl.BlockSpec((rb, half), lambda nb, be, nu: (jnp.minimum(nb, nu[0]), 0)),
        scratch_shapes=[
            pltpu.VMEM((3, rb, half), U32),
            pltpu.VMEM((d, f2), BF16),
            pltpu.VMEM((f, d), BF16),
            pltpu.SMEM((MOE_ID_RING * rb,), jnp.int32),
            pltpu.SemaphoreType.DMA((3,)),
            pltpu.SemaphoreType.DMA,
        ],
    )
    return pl.pallas_call(
        kern,
        grid_spec=grid_spec,
        out_shape=jax.ShapeDtypeStruct(((n_blocks + 1) * rb, half), U32),
        compiler_params=_params("arbitrary"),
        name="moe_experts",
    )(block_expert, n_used, tok, gate, w_in, w_out, hp)


def _moe_combine_kernel(pos_hbm, y_hbm, h_ref, wsi_ref, wso_ref, x_ref, gl_ref, gc_ref, o_ref,
                        ybuf, pos_smem, gsem, isem, *, lay, f, steps):
    rt = COMBINE_ROWS
    n_rows = TOP_K * rt
    i = pl.program_id(0)
    s = i % 2
    half = h_ref.shape[-1]

    def ring(step):
        return pl.multiple_of((step % COMBINE_POS_RING) * n_rows, n_rows)

    def pos_copy(step):
        return pltpu.make_async_copy(pos_hbm.at[step], pos_smem.at[pl.ds(ring(step), n_rows)], isem)

    def gather_row(k, r, slot):
        return pltpu.make_async_copy(y_hbm.at[pl.ds(pos_smem[k + r], 1)], ybuf.at[slot, pl.ds(r, 1)],
                                     gsem.at[slot])

    def gather_wait(slot):
        pltpu.make_async_copy(y_hbm.at[pl.ds(0, n_rows)], ybuf.at[slot], gsem.at[slot]).wait()

    @pl.when(i == 0)
    def _():
        for step in (0, 1):
            c = pos_copy(step)
            c.start()
            c.wait()

        def body(r, carry):
            gather_row(0, r, 0).start()
            return carry
        lax.fori_loop(0, n_rows, body, 0)

    @pl.when(i > 0)
    def _():
        pos_copy(i + 1).wait()

    pos_copy(i + 2).start()
    nk = ring(i + 1)
    gather_wait(s)
    lo = hi = None
    for k in range(TOP_K):
        for r in range(k * rt, (k + 1) * rt):
            gather_row(nk, r, 1 - s).start(priority=r % 2)
        l, h = _unpack_halves(ybuf[s, k * rt:(k + 1) * rt, :])
        lo, hi = (l, h) if lo is None else (lo + l, hi + h)
    hlo, hhi = _unpack_halves(h_ref[...])
    hu = (jnp.dot(hlo.astype(BF16), wsi_ref[:half, :], preferred_element_type=F32)
          + jnp.dot(hhi.astype(BF16), wsi_ref[half:, :], preferred_element_type=F32))
    act = (jax.nn.silu(hu[:, :f]) * hu[:, f:]).astype(BF16)
    shared = jnp.dot(act, wso_ref[...], preferred_element_type=F32)
    is_ctx = (pl.program_id(0) * COMBINE_ROWS) % lay.p < lay.ctx_len
    gate = jnp.where(is_ctx, gc_ref[0], gl_ref[0])
    o_ref[:, :half] = x_ref[:, :half] + gate[:, :half] * (lo + shared[:, :half])
    o_ref[:, half:] = x_ref[:, half:] + gate[:, half:] * (hi + shared[:, half:])

    @pl.when(i == steps - 1)
    def _():
        gather_wait(1 - s)
        pos_copy(i + 2).wait()


def _moe_combine(lay, y, pos, hp, ws_in, ws_out, x, mods, gate_idx):
    d = lay.d
    half = d // 2
    rt = COMBINE_ROWS
    assert lay.ctx_len % rt == 0 and lay.p % rt == 0
    steps = lay.t // rt
    assert pos.shape == (steps + 2, TOP_K * rt)
    f = ws_out.shape[0]
    return pl.pallas_call(
        functools.partial(_moe_combine_kernel, lay=lay, f=f, steps=steps),
        grid=(steps,),
        in_specs=[
            pl.BlockSpec(memory_space=pl.ANY),
            pl.BlockSpec(memory_space=pl.ANY),
            pl.BlockSpec((rt, half), lambda i: (i, 0)),
            pl.BlockSpec((d, 2 * f), lambda i: (0, 0)),
            pl.BlockSpec((f, d), lambda i: (0, 0)),
            pl.BlockSpec((rt, d), lambda i: (i, 0)),
            pl.BlockSpec((1, 1, d), lambda i: (i * rt // lay.p, 0, gate_idx)),
            pl.BlockSpec((1, 1, d), lambda i: (lay.batch, 0, gate_idx)),
        ],
        out_specs=pl.BlockSpec((rt, d), lambda i: (i, 0)),
        out_shape=jax.ShapeDtypeStruct((lay.t, d), F32),
        scratch_shapes=[
            pltpu.VMEM((2, TOP_K * rt, half), U32),
            pltpu.SMEM((COMBINE_POS_RING * TOP_K * rt,), jnp.int32),
            pltpu.SemaphoreType.DMA((2,)),
            pltpu.SemaphoreType.DMA,
        ],
        input_output_aliases={5: 0},
        compiler_params=_params("arbitrary"),
        name="moe_combine",
    )(pos, y, hp, ws_in, ws_out, x, mods, mods)


def _final_norm_kernel(x_ref, w_ref, o_ref):
    x = x_ref[0]
    o_ref[0] = x * lax.rsqrt(jnp.mean(x * x, axis=-1, keepdims=True) + EPS) * w_ref[...]


def _final_norm(lay, x, w):
    d = lay.d
    rows = _pick(lay.seq, (256, 128, 64))
    assert lay.ctx_len % rows == 0
    off = lay.ctx_len // rows
    return pl.pallas_call(
        _final_norm_kernel,
        grid=(lay.batch, lay.seq // rows),
        in_specs=[pl.BlockSpec((1, rows, d), lambda b, j: (b, off + j, 0)),
                  pl.BlockSpec((1, d), lambda b, j: (0, 0))],
        out_specs=pl.BlockSpec((1, rows, d), lambda b, j: (b, j, 0)),
        out_shape=jax.ShapeDtypeStruct((lay.batch, lay.seq, d), F32),
        compiler_params=_params("arbitrary", "arbitrary"),
        name="final_norm",
    )(x.reshape(lay.batch, lay.p, d), w)


def kernel(x, c, ctx, c_ctx, ada_w_down, ada_w_up, ada_b, norm_mix, norm_ffn, conv_w_in, conv_w, conv_w_out, gla_w_in, gla_w_a2_fwd, gla_b_a2_fwd, gla_w_a2_bwd, gla_b_a2_bwd, gla_norm, gla_w_out, router_w, router_bias, exp_w_in, exp_w_out, shared_w_in, shared_w_out, norm_final):
    batch, seq, d = x.shape
    ctx_len = ctx.shape[1]
    depth = ada_w_down.shape[0]
    lay = _Layout(batch, ctx_len, seq, d)
    qk = d // 2
    tn = _pick(d, (512, 256, 128))

    cond = jnp.zeros((MOD_ROWS, d), F32).at[:batch].set(c).at[batch].set(c_ctx)
    mods_all = _ada_all(cond, ada_w_down, ada_w_up, ada_b)
    xs = jnp.concatenate([ctx, x], axis=1).reshape(lay.t, d)

    for i in range(depth):
        j = i // 2
        mods = mods_all[i].reshape(MOD_ROWS, 1, N_MOD * d)
        nw_mix = norm_mix[i].reshape(1, d)
        if i % 2 == 0:
            y = _norm_proj(lay, xs, nw_mix, mods, 0, 1, conv_w_in[j].astype(BF16), out_dtype=BF16,
                           tn=_pick(d, (256, 128)), conv_w=conv_w[j], name="conv_in")
            xs = _out_proj(lay, y, conv_w_out[j].astype(BF16), xs, mods, 2, tn=tn, name="conv_out")
        else:
            n_main = 2 * qk + 2 * d
            w_aux = jnp.zeros((d, LANES), BF16).at[:, :2 * GLA_GATE_RANK].set(
                gla_w_in[j][:, n_main:].astype(BF16))
            qkvr, a = _norm_proj(lay, xs, nw_mix, mods, 0, 1, gla_w_in[j].astype(BF16), out_dtype=BF16,
                                 tn=tn, n=n_main, w_aux=w_aux, name="gla_in")
            waf = jnp.zeros((LANES, qk), BF16).at[:GLA_GATE_RANK].set(gla_w_a2_fwd[j].astype(BF16))
            wab = jnp.zeros((LANES, qk), BF16).at[GLA_GATE_RANK:2 * GLA_GATE_RANK].set(
                gla_w_a2_bwd[j].astype(BF16))
            y = _gla(lay, qkvr, a, waf, gla_b_a2_fwd[j].reshape(1, qk), wab,
                     gla_b_a2_bwd[j].reshape(1, qk), gla_norm[j].reshape(1, d // GLA_HEADS))
            xs = _out_proj(lay, y, gla_w_out[j].astype(BF16), xs, mods, 2, tn=tn, name="gla_out")

        rw = jnp.zeros((d, LANES), BF16).at[:, :N_EXPERTS].set(router_w[i].astype(BF16))
        rb = jnp.zeros((1, LANES), F32).at[0, :N_EXPERTS].set(router_bias[i])
        hp, gates, ids = _router(lay, xs, norm_ffn[i].reshape(1, d), mods, rw, rb)
        tok, gate, block_expert, n_used, pos = _moe_plan(lay, gates, ids)
        y = _moe_experts(lay, i, hp, tok, gate, block_expert, n_used, exp_w_in, exp_w_out)
        xs = _moe_combine(lay, y, pos, hp, shared_w_in[i].astype(BF16), shared_w_out[i].astype(BF16),
                          xs, mods, 5)

    return _final_norm(lay, xs, norm_final.reshape(1, d))
```

```python
import functools

import jax
import jax.numpy as jnp
import numpy as np
from jax import lax
from jax.experimental import pallas as pl
from jax.experimental.pallas import tpu as pltpu

GRID_W = 64
EPS = 1e-6
N_MOD = 6
GLA_HEADS = 8
GLA_GATE_RANK = 16
GLA_GATE_TAU = 16.0
GLA_LOG_DECAY_FLOOR = -1.0
N_EXPERTS = 64
N_GROUPS = 8
TOPK_GROUPS = 4
TOP_K = 8
ROUTED_SCALE = 2.5

LANES = 128
GLA_ROWS = 128
TILES_PER_SAMPLE = 4
MOD_ROWS = 16
VMEM_LIMIT_BYTES = 56 * 1024 * 1024

MOE_ROWS = 256
COMBINE_ROWS = 128
MOE_EXTRA_BLOCKS = 5
MOE_ID_RING = 8
COMBINE_POS_RING = 4

BF16 = jnp.bfloat16
F32 = jnp.float32
U32 = jnp.uint32
HIGH_HALF = np.uint32(0xFFFF0000)
NEG_INF = float("-inf")


def _params(*sem):
    return pltpu.CompilerParams(dimension_semantics=sem, vmem_limit_bytes=VMEM_LIMIT_BYTES)


def _pick(n, prefs):
    for p in prefs:
        if n % p == 0:
            return p
    return n


def _ada_kernel(cond_ref, wd_ref, wu_ref, b_ref, o_ref, z_ref):
    @pl.when(pl.program_id(1) == 0)
    def _():
        s = jax.nn.silu(cond_ref[...]).astype(BF16)
        z_ref[...] = jnp.dot(s, wd_ref[0].astype(BF16), preferred_element_type=F32)

    o_ref[0] = jnp.dot(z_ref[...].astype(BF16), wu_ref[0].astype(BF16),
                       preferred_element_type=F32) + b_ref[0]


def _ada_all(cond, w_down, w_up, b):
    depth, d, r = w_down.shape
    n = w_up.shape[-1]
    tn = _pick(n, (2048, 1024, 512, 256, 128))
    return pl.pallas_call(
        _ada_kernel,
        grid=(depth, n // tn),
        in_specs=[
            pl.BlockSpec((MOD_ROWS, d), lambda l, j: (0, 0)),
            pl.BlockSpec((1, d, r), lambda l, j: (l, 0, 0)),
            pl.BlockSpec((1, r, tn), lambda l, j: (l, 0, j)),
            pl.BlockSpec((1, 1, tn), lambda l, j: (l, 0, j)),
        ],
        out_specs=pl.BlockSpec((1, MOD_ROWS, tn), lambda l, j: (l, 0, j)),
        out_shape=jax.ShapeDtypeStruct((depth, MOD_ROWS, n), F32),
        scratch_shapes=[pltpu.VMEM((MOD_ROWS, r), F32)],
        compiler_params=_params("arbitrary", "arbitrary"),
        name="ada_mod",
    )(cond, w_down, w_up, b.reshape(depth, 1, n))


class _Layout:
    def __init__(self, batch, ctx_len, seq, d):
        self.batch, self.ctx_len, self.seq, self.d = batch, ctx_len, seq, d
        self.p = ctx_len + seq
        self.t = batch * self.p
        assert self.p % TILES_PER_SAMPLE == 0
        self.tm = self.p // TILES_PER_SAMPLE
        assert GRID_W & (GRID_W - 1) == 0
        assert self.tm % GRID_W == 0 and ctx_len % GRID_W == 0 and self.tm % 16 == 0
        assert ctx_len <= self.tm and batch < MOD_ROWS
        self.n_tiles = self.t // self.tm
        self.slab = _pick(self.tm, (32, 16))
        assert ctx_len % self.slab == 0


def _mod_specs(lay, idx, two_axes=True):
    d = lay.d
    if two_axes:
        lat = pl.BlockSpec((1, 1, d), lambda i, j: (i // TILES_PER_SAMPLE, 0, idx))
        ctx = pl.BlockSpec((1, 1, d), lambda i, j: (lay.batch, 0, idx))
    else:
        lat = pl.BlockSpec((1, 1, d), lambda i: (i // TILES_PER_SAMPLE, 0, idx))
        ctx = pl.BlockSpec((1, 1, d), lambda i: (lay.batch, 0, idx))
    return lat, ctx


def _tile_row0(lay):
    return (pl.program_id(0) % TILES_PER_SAMPLE) * lay.tm


def _row_in_sample(lay, shape):
    return _tile_row0(lay) + lax.broadcasted_iota(jnp.int32, shape, 0)


def _pack_halves(x):
    half = x.shape[1] // 2
    xb = x.astype(BF16).astype(F32)
    lo = lax.bitcast_convert_type(xb[:, :half], U32) >> 16
    hi = lax.bitcast_convert_type(xb[:, half:], U32) & HIGH_HALF
    return hi | lo


def _unpack_halves(w):
    lo = lax.bitcast_convert_type(w << 16, F32)
    hi = lax.bitcast_convert_type(w & HIGH_HALF, F32)
    return lo, hi


def _modnorm_rows(lay, x_ref, nw_ref, shl_ref, scl_ref, shc_ref, scc_ref, h_ref, packed_ref=None):
    nw = nw_ref[...]
    slab = lay.slab
    row0 = _tile_row0(lay)

    def body(s, carry):
        r = pl.multiple_of(s * slab, slab)
        x = x_ref[pl.ds(r, slab), :]
        ms = jnp.mean(x * x, axis=-1, keepdims=True)
        y = x * lax.rsqrt(ms + EPS) * nw
        is_ctx = row0 + r < lay.ctx_len
        sc = jnp.where(is_ctx, scc_ref[0], scl_ref[0])
        sh = jnp.where(is_ctx, shc_ref[0], shl_ref[0])
        h = y * (1.0 + sc) + sh
        h_ref[pl.ds(r, slab), :] = h.astype(h_ref.dtype)
        if packed_ref is not None:
            packed_ref[pl.ds(r, slab), :] = _pack_halves(h)
        return carry

    lax.fori_loop(0, lay.tm // slab, body, 0)


def _proj_kernel(x_ref, nw_ref, shl, scl, shc, scc, w_ref, o_ref, h_ref, *, lay):
    @pl.when(pl.program_id(1) == 0)
    def _():
        _modnorm_rows(lay, x_ref, nw_ref, shl, scl, shc, scc, h_ref)

    o_ref[...] = jnp.dot(h_ref[...], w_ref[...], preferred_element_type=F32).astype(o_ref.dtype)


def _proj_aux_kernel(x_ref, nw_ref, shl, scl, shc, scc, w_ref, wa_ref, o_ref, a_ref, h_ref, *, lay):
    @pl.when(pl.program_id(1) == 0)
    def _():
        _modnorm_rows(lay, x_ref, nw_ref, shl, scl, shc, scc, h_ref)
        a_ref[...] = jnp.dot(h_ref[...], wa_ref[...], preferred_element_type=F32)

    o_ref[...] = jnp.dot(h_ref[...], w_ref[...], preferred_element_type=F32).astype(o_ref.dtype)


def _conv_proj_kernel(x_ref, nw_ref, shl, scl, shc, scc, wb_ref, wc_ref, wv_ref, cw_ref, o_ref, h_ref,
                      *, lay, tc):
    @pl.when(pl.program_id(1) == 0)
    def _():
        _modnorm_rows(lay, x_ref, nw_ref, shl, scl, shc, scc, h_ref)

    tm = lay.tm
    h = h_ref[...]
    bg = jnp.dot(h, wb_ref[...], preferred_element_type=F32)
    cg = jnp.dot(h, wc_ref[...], preferred_element_type=F32)
    v = jnp.dot(h, wv_ref[...], preferred_element_type=F32)
    u = cg * v
    row = _row_in_sample(lay, (tm, tc))
    is_ctx = row < lay.ctx_len
    col = (row - lay.ctx_len) & (GRID_W - 1)
    first = (is_ctx & (row == 0)) | (~is_ctx & (col == 0))
    last = (is_ctx & (row == lay.ctx_len - 1)) | (~is_ctx & (col == GRID_W - 1))
    prev = jnp.where(first, 0.0, pltpu.roll(u, 1, axis=0))
    nxt = jnp.where(last, 0.0, pltpu.roll(u, tm - 1, axis=0))
    cw = cw_ref[...]
    conv = prev * cw[0:1, :] + u * cw[1:2, :] + nxt * cw[2:3, :]
    o_ref[...] = (bg * conv).astype(o_ref.dtype)


def _norm_proj(lay, x, nw, mods, shift_idx, scale_idx, w, *, out_dtype, tn, n=None, w_aux=None,
               conv_w=None, name):
    d = w.shape[0]
    n = w.shape[1] if n is None else n
    assert n % tn == 0
    shl, shc = _mod_specs(lay, shift_idx)
    scl, scc = _mod_specs(lay, scale_idx)
    in_specs = [
        pl.BlockSpec((lay.tm, d), lambda i, j: (i, 0)),
        pl.BlockSpec((1, d), lambda i, j: (0, 0)),
        shl, scl, shc, scc,
        pl.BlockSpec((d, tn), lambda i, j: (0, j)),
    ]
    args = [x, nw, mods, mods, mods, mods, w]
    scratch = [pltpu.VMEM((lay.tm, d), BF16)]
    params = _params("arbitrary", "arbitrary")
    if conv_w is not None:
        tc = tn
        nct = n // 3 // tc
        in_specs[-1:] = [pl.BlockSpec((d, tc), functools.partial(lambda i, j, k: (0, k * nct + j), k=k))
                         for k in range(3)]
        in_specs.append(pl.BlockSpec((3, tc), lambda i, j: (0, j)))
        return pl.pallas_call(
            functools.partial(_conv_proj_kernel, lay=lay, tc=tc),
            grid=(lay.n_tiles, nct), in_specs=in_specs,
            out_specs=pl.BlockSpec((lay.tm, tc), lambda i, j: (i, j)),
            out_shape=jax.ShapeDtypeStruct((lay.t, n // 3), out_dtype),
            scratch_shapes=scratch, compiler_params=params, name=name,
        )(*args[:-1], w, w, w, conv_w)
    if w_aux is not None:
        na = w_aux.shape[1]
        in_specs.append(pl.BlockSpec((d, na), lambda i, j: (0, 0)))
        return pl.pallas_call(
            functools.partial(_proj_aux_kernel, lay=lay),
            grid=(lay.n_tiles, n // tn), in_specs=in_specs,
            out_specs=[pl.BlockSpec((lay.tm, tn), lambda i, j: (i, j)),
                       pl.BlockSpec((lay.tm, na), lambda i, j: (i, 0))],
            out_shape=[jax.ShapeDtypeStruct((lay.t, n), out_dtype),
                       jax.ShapeDtypeStruct((lay.t, na), F32)],
            scratch_shapes=scratch, compiler_params=params, name=name,
        )(*args, w_aux)
    return pl.pallas_call(
        functools.partial(_proj_kernel, lay=lay),
        grid=(lay.n_tiles, n // tn), in_specs=in_specs,
        out_specs=pl.BlockSpec((lay.tm, tn), lambda i, j: (i, j)),
        out_shape=jax.ShapeDtypeStruct((lay.t, n), out_dtype),
        scratch_shapes=scratch, compiler_params=params, name=name,
    )(*args)


def _out_proj_kernel(a_ref, w_ref, x_ref, gl_ref, gc_ref, o_ref, *, lay):
    y = jnp.dot(a_ref[...], w_ref[...], preferred_element_type=F32)
    is_ctx = _row_in_sample(lay, y.shape) < lay.ctx_len
    gate = jnp.where(is_ctx, gc_ref[0], gl_ref[0])
    o_ref[...] = x_ref[...] + gate * y


def _out_proj(lay, a, w, x, mods, gate_idx, *, tn, name):
    k, d = w.shape
    gl = pl.BlockSpec((1, 1, tn), lambda i, j: (i // TILES_PER_SAMPLE, 0, gate_idx * (d // tn) + j))
    gc = pl.BlockSpec((1, 1, tn), lambda i, j: (lay.batch, 0, gate_idx * (d // tn) + j))
    return pl.pallas_call(
        functools.partial(_out_proj_kernel, lay=lay),
        grid=(lay.n_tiles, d // tn),
        in_specs=[
            pl.BlockSpec((lay.tm, k), lambda i, j: (i, 0)),
            pl.BlockSpec((k, tn), lambda i, j: (0, j)),
            pl.BlockSpec((lay.tm, tn), lambda i, j: (i, j)),
            gl, gc,
        ],
        out_specs=pl.BlockSpec((lay.tm, tn), lambda i, j: (i, j)),
        out_shape=jax.ShapeDtypeStruct((lay.t, d), F32),
        input_output_aliases={2: 0},
        compiler_params=_params("arbitrary", "arbitrary"),
        name=name,
    )(a, w, x, mods, mods)


def _gla_kernel(q_ref, k_ref, v_ref, r_ref, a_ref, waf_ref, baf_ref, wab_ref, bab_ref, nw_ref,
                y_ref, of_ref, ob_ref, sf_ref, sb_ref, *, n_chunks, n_ctx_chunks, dk):
    c_rows = GLA_ROWS
    scale = dk ** -0.5
    ri = lax.broadcasted_iota(jnp.int32, (c_rows, c_rows), 0)
    ci = lax.broadcasted_iota(jnp.int32, (c_rows, c_rows), 1)
    nt = (((1,), (1,)), ((), ()))
    tn_dims = (((0,), (0,)), ((), ()))

    def chunk(c, forward, s_ref):
        rows = pl.ds(pl.multiple_of(c * c_rows, c_rows), c_rows)
        keep = (ci <= ri) if forward else (ci >= ri)
        tri = keep.astype(BF16)
        wa, ba = (waf_ref, baf_ref) if forward else (wab_ref, bab_ref)
        z = jnp.dot(a_ref[rows, :].astype(BF16), wa[...], preferred_element_type=F32) + ba[...]
        g = jnp.maximum(jax.nn.log_sigmoid(z) / GLA_GATE_TAU, GLA_LOG_DECAY_FLOOR)
        g1 = g.astype(BF16)
        rem = g - g1.astype(F32)
        g2 = rem.astype(BF16)
        g3 = (rem - g2.astype(F32)).astype(BF16)
        bcum = (jnp.dot(tri, g1, preferred_element_type=F32)
                + jnp.dot(tri, g2, preferred_element_type=F32)
                + jnp.dot(tri, g3, preferred_element_type=F32))
        mid = bcum[c_rows // 2:c_rows // 2 + 1, :]
        tot = bcum[c_rows - 1:c_rows, :] if forward else bcum[0:1, :]
        q = q_ref[rows, :].astype(F32) * scale
        k = k_ref[rows, :].astype(F32)
        v = v_ref[rows, :]
        q_rel = (q * jnp.exp(bcum - mid)).astype(BF16)
        k_rel = (k * jnp.exp(mid - bcum)).astype(BF16)
        q_dec = (q * jnp.exp(bcum)).astype(BF16)
        k_end = (k * jnp.exp(tot - bcum)).astype(BF16)
        att = lax.dot_general(q_rel, k_rel, nt, preferred_element_type=F32)
        att = jnp.where(keep, att, 0.0).astype(BF16)
        s = s_ref[...]
        o = (jnp.dot(att, v, preferred_element_type=F32)
             + lax.dot_general(q_dec, s.astype(BF16), nt, preferred_element_type=F32))
        s_ref[...] = jnp.exp(tot) * s + lax.dot_general(v, k_end, tn_dims, preferred_element_type=F32)
        return rows, o

    sf_ref[...] = jnp.zeros_like(sf_ref)
    sb_ref[...] = jnp.zeros_like(sb_ref)

    def scan_body(n, carry):
        rows, o = chunk(n, True, sf_ref)
        of_ref[rows, :] = o
        c = jnp.where(n < n_ctx_chunks, n_ctx_chunks - 1 - n, n_chunks - 1 - (n - n_ctx_chunks))
        rows, o = chunk(c, False, sb_ref)
        ob_ref[rows, :] = o
        return carry

    lax.fori_loop(0, n_chunks, scan_body, 0)
    nw = nw_ref[...]

    def out_body(n, carry):
        rows = pl.ds(pl.multiple_of(n * c_rows, c_rows), c_rows)
        o = of_ref[rows, :] + ob_ref[rows, :]
        o = o * lax.rsqrt(jnp.mean(o * o, axis=-1, keepdims=True) + EPS) * nw
        y_ref[rows, :] = (o * jax.nn.silu(r_ref[rows, :].astype(F32))).astype(y_ref.dtype)
        return carry

    lax.fori_loop(0, n_chunks, out_body, 0)


def _gla(lay, qkvr, a, waf, baf, wab, bab, norm_w):
    d = lay.d
    dk = d // 2 // GLA_HEADS
    dv = d // GLA_HEADS
    assert lay.ctx_len % GLA_ROWS == 0 and lay.seq % GLA_ROWS == 0
    n_chunks = lay.p // GLA_ROWS
    kern = functools.partial(_gla_kernel, n_chunks=n_chunks,
                             n_ctx_chunks=lay.ctx_len // GLA_ROWS, dk=dk)
    h = GLA_HEADS
    return pl.pallas_call(
        kern,
        grid=(lay.batch, h),
        in_specs=[
            pl.BlockSpec((lay.p, dk), lambda b, i: (b, i)),
            pl.BlockSpec((lay.p, dk), lambda b, i: (b, h + i)),
            pl.BlockSpec((lay.p, dv), lambda b, i: (b, h + i)),
            pl.BlockSpec((lay.p, dv), lambda b, i: (b, 2 * h + i)),
            pl.BlockSpec((lay.p, LANES), lambda b, i: (b, 0)),
            pl.BlockSpec((LANES, dk), lambda b, i: (0, i)),
            pl.BlockSpec((1, dk), lambda b, i: (0, i)),
            pl.BlockSpec((LANES, dk), lambda b, i: (0, i)),
            pl.BlockSpec((1, dk), lambda b, i: (0, i)),
            pl.BlockSpec((1, dv), lambda b, i: (0, 0)),
        ],
        out_specs=pl.BlockSpec((lay.p, dv), lambda b, i: (b, i)),
        out_shape=jax.ShapeDtypeStruct((lay.t, d), BF16),
        scratch_shapes=[pltpu.VMEM((lay.p, dv), F32), pltpu.VMEM((lay.p, dv), F32),
                        pltpu.VMEM((dv, dk), F32), pltpu.VMEM((dv, dk), F32)],
        compiler_params=_params("arbitrary", "arbitrary"),
        name="gla_scan",
    )(qkvr, qkvr, qkvr, qkvr, a, waf, baf, wab, bab, norm_w)


def _route(logits, bias):
    n_rows = logits.shape[0]
    per_group = N_EXPERTS // N_GROUPS
    lane_i = lax.broadcasted_iota(jnp.int32, (n_rows, LANES), 1)
    lane = lane_i.astype(F32)
    group = (lane_i // per_group).astype(F32)
    valid = lane_i < N_EXPERTS
    big = float(2 * LANES)
    scores = jax.nn.sigmoid(logits)
    sel = jnp.where(valid, scores + bias, NEG_INF)

    def partner(x, s):
        return jnp.where((lane_i & s) == 0, pltpu.roll(x, LANES - s, axis=1), pltpu.roll(x, s, axis=1))

    def group_reduce(x, op):
        s = 1
        while s < per_group:
            x = op(x, partner(x, s))
            s *= 2
        return x

    m1 = group_reduce(sel, jnp.maximum)
    first = group_reduce(jnp.where(sel == m1, lane, big), jnp.minimum)
    m2 = group_reduce(jnp.where(lane == first, NEG_INF, sel), jnp.maximum)
    cur = jnp.where(valid, m1 + m2, NEG_INF)

    def pick_best(cur, ident):
        m = jnp.max(cur, axis=1, keepdims=True)
        best = jnp.min(jnp.where(cur == m, ident, big), axis=1, keepdims=True)
        return ident == best

    group_ok = jnp.zeros((n_rows, LANES), jnp.bool_)
    for _ in range(TOPK_GROUPS):
        p = pick_best(cur, group)
        group_ok = group_ok | p
        cur = jnp.where(p, NEG_INF, cur)

    cur = jnp.where(group_ok & valid, sel, NEG_INF)
    picked = jnp.zeros((n_rows, LANES), jnp.bool_)
    ids = jnp.zeros((n_rows, LANES), F32)
    for it in range(TOP_K):
        m = jnp.max(cur, axis=1, keepdims=True)
        best = jnp.min(jnp.where(cur == m, lane, big), axis=1, keepdims=True)
        p = lane == best
        picked = picked | p
        cur = jnp.where(p, NEG_INF, cur)
        ids = jnp.where(lane_i == it, best, ids)

    w = jnp.where(picked, scores, 0.0)
    gates = w / jnp.sum(w, axis=1, keepdims=True) * ROUTED_SCALE
    return gates, ids.astype(jnp.int32)


def _router_kernel(x_ref, nw_ref, shl, scl, shc, scc, rw_ref, rb_ref, hp_ref, g_ref, id_ref, h_ref, *, lay):
    _modnorm_rows(lay, x_ref, nw_ref, shl, scl, shc, scc, h_ref, packed_ref=hp_ref)
    logits = jnp.dot(h_ref[...], rw_ref[...], preferred_element_type=F32)
    g_ref[...], id_ref[...] = _route(logits, rb_ref[...])


def _router(lay, x, nw, mods, rw, rb):
    d = lay.d
    shl, shc = _mod_specs(lay, 3, two_axes=False)
    scl, scc = _mod_specs(lay, 4, two_axes=False)
    return pl.pallas_call(
        functools.partial(_router_kernel, lay=lay),
        grid=(lay.n_tiles,),
        in_specs=[
            pl.BlockSpec((lay.tm, d), lambda i: (i, 0)),
            pl.BlockSpec((1, d), lambda i: (0, 0)),
            shl, scl, shc, scc,
            pl.BlockSpec((d, LANES), lambda i: (0, 0)),
            pl.BlockSpec((1, LANES), lambda i: (0, 0)),
        ],
        out_specs=[pl.BlockSpec((lay.tm, d // 2), lambda i: (i, 0)),
                   pl.BlockSpec((lay.tm, LANES), lambda i: (i, 0)),
                   pl.BlockSpec((lay.tm, LANES), lambda i: (i, 0))],
        out_shape=[jax.ShapeDtypeStruct((lay.t, d // 2), U32),
                   jax.ShapeDtypeStruct((lay.t, LANES), F32),
                   jax.ShapeDtypeStruct((lay.t, LANES), jnp.int32)],
        scratch_shapes=[pltpu.VMEM((lay.tm, d), BF16)],
        compiler_params=_params("arbitrary"),
        name="moe_router",
    )(x, nw, mods, mods, mods, mods, rw, rb)


def _moe_plan(lay, gates, ids):
    t, rb = lay.t, MOE_ROWS
    n_pairs = t * TOP_K
    n_blocks = n_pairs // rb + N_EXPERTS
    assert N_EXPERTS * n_pairs < 2 ** 31
    eidx = ids[:, :TOP_K]
    w = jnp.take_along_axis(gates, eidx, axis=1).reshape(-1)
    key = eidx.reshape(-1) * n_pairs + jnp.arange(n_pairs, dtype=jnp.int32)
    key_s, w_s = lax.sort((key, w), num_keys=1)
    bounds = jnp.searchsorted(key_s, jnp.arange(N_EXPERTS + 1, dtype=jnp.int32) * n_pairs).astype(jnp.int32)
    start, counts = bounds[:-1], bounds[1:] - bounds[:-1]
    padded = (counts + rb - 1) // rb * rb
    pad_end = jnp.cumsum(padded)
    pad_start = pad_end - padded
    blk = jnp.arange(n_blocks + MOE_EXTRA_BLOCKS, dtype=jnp.int32) - 1
    e_blk = jnp.clip(jnp.searchsorted(pad_end, blk * rb, side="right", method="compare_all"),
                     0, N_EXPERTS - 1)
    r = jnp.arange(rb, dtype=jnp.int32)[None, :]
    in_run = (blk * rb - pad_start[e_blk])[:, None] + r
    valid = ((blk >= 0) & (blk * rb < pad_end[-1]))[:, None] & (in_run < counts[e_blk][:, None])
    src = jnp.clip(start[e_blk][:, None] + in_run, 0, n_pairs - 1)
    pair = key_s[src] % n_pairs
    tok = jnp.where(valid, pair // TOP_K, r)
    gate = jnp.where(valid, w_s[src], 0.0)[1:n_blocks + 1].reshape(n_blocks * rb, 1)
    n_used = pad_end[-1] // rb
    step_blk = jnp.minimum(jnp.arange(n_blocks + 1, dtype=jnp.int32), n_used - 1)
    block_expert = jnp.searchsorted(pad_end, step_blk * rb, side="right",
                                    method="compare_all").astype(jnp.int32)
    chose = jnp.sum(eidx[:, :, None] == jnp.arange(N_EXPERTS, dtype=jnp.int32)[None, None, :], axis=1,
                    dtype=jnp.int32)
    earlier = jnp.cumsum(chose, axis=0) - chose
    pos = pad_start[eidx] + jnp.take_along_axis(earlier, eidx, axis=1)
    rt = COMBINE_ROWS
    pos = pos.reshape(t // rt, rt, TOP_K).transpose(0, 2, 1).reshape(t // rt, TOP_K * rt)
    pos = jnp.concatenate([pos, jnp.zeros((2, TOP_K * rt), pos.dtype)]).astype(jnp.int32)
    return tok.astype(jnp.int32), gate, block_expert, n_used.reshape(1).astype(jnp.int32), pos


def _moe_expert_kernel(be_ref, nu_ref, tok_hbm, g_ref, win_ref, wout_ref, h_hbm, y_ref,
                       hbuf, winb, woutb, tok_smem, gsem, isem, *, f):
    nb = pl.program_id(0)
    pl.when(nb <= nu_ref[0])(functools.partial(
        _moe_expert_step, nb, nu_ref[0], be_ref, tok_hbm, g_ref, win_ref, wout_ref, h_hbm, y_ref,
        hbuf, winb, woutb, tok_smem, gsem, isem, f=f))


def _moe_expert_step(nb, n_used, be_ref, tok_hbm, g_ref, win_ref, wout_ref, h_hbm, y_ref,
                     hbuf, winb, woutb, tok_smem, gsem, isem, *, f):
    rb = MOE_ROWS
    hs = nb % 3
    half = hbuf.shape[-1]

    def ring(block):
        return pl.multiple_of(((block + 1) % MOE_ID_RING) * rb, rb)

    def ids_copy(block):
        return pltpu.make_async_copy(tok_hbm.at[block + 1], tok_smem.at[pl.ds(ring(block), rb)], isem)

    def gather_row(k, r, s):
        return pltpu.make_async_copy(h_hbm.at[pl.ds(tok_smem[k + r], 1)], hbuf.at[s, pl.ds(r, 1)],
                                     gsem.at[s])

    def gather_wait(s):
        pltpu.make_async_copy(h_hbm.at[pl.ds(0, rb)], hbuf.at[s], gsem.at[s]).wait()

    @pl.when(nb == 0)
    def _():
        for block in (0, 1, 2):
            c = ids_copy(block)
            c.start()
            c.wait()
        for block in (0, 1):
            def body(r, carry, block=block):
                gather_row((block + 1) * rb, r, block).start()
                return carry
            lax.fori_loop(0, rb, body, 0)

    @pl.when(nb > 0)
    def _():
        ids_copy(nb + 2).wait()

    ids_copy(nb + 3).start()

    @pl.when((nb == 0) | (be_ref[nb] != be_ref[jnp.maximum(nb - 1, 0)]))
    def _():
        winb[...] = win_ref[0, 0].astype(BF16)
        woutb[...] = wout_ref[0, 0].astype(BF16)

    gk = ring(nb + 2)
    ghs = (nb + 2) % 3
    n_groups = 8
    per_group = rb // n_groups

    def issue(group):
        for r in range(group * per_group, (group + 1) * per_group):
            gather_row(gk, r, ghs).start(priority=r % 2)

    gather_wait(hs)
    issue(0)
    cols = half // 4
    hu = jnp.zeros((rb, 2 * f), F32)
    for piece in range(4):
        c0 = piece * cols
        lo, hi = _unpack_halves(hbuf[hs, :, c0:c0 + cols])
        hu = (hu + jnp.dot(lo.astype(BF16), winb[c0:c0 + cols, :], preferred_element_type=F32)
              + jnp.dot(hi.astype(BF16), winb[half + c0:half + c0 + cols, :],
                        preferred_element_type=F32))
        issue(1 + piece)
    act = (jax.nn.silu(hu[:, :f]) * hu[:, f:] * g_ref[...]).astype(BF16)
    for piece in range(4):
        c0 = piece * cols
        y_lo = jnp.dot(act, woutb[:, c0:c0 + cols], preferred_element_type=F32)
        y_hi = jnp.dot(act, woutb[:, half + c0:half + c0 + cols], preferred_element_type=F32)
        y_ref[:, c0:c0 + cols] = _pack_halves(jnp.concatenate([y_lo, y_hi], axis=1))
        if piece < 3:
            issue(5 + piece)

    @pl.when(nb == n_used)
    def _():
        gather_wait((nb + 1) % 3)
        gather_wait((nb + 2) % 3)
        ids_copy(nb + 3).wait()


def _moe_experts(lay, layer, hp, tok, gate, block_expert, n_used, w_in, w_out):
    _, ne, d, f2 = w_in.shape
    f = f2 // 2
    rb = MOE_ROWS
    half = d // 2
    n_blocks = gate.shape[0] // rb
    assert tok.shape == (n_blocks + MOE_EXTRA_BLOCKS, rb) and rb <= lay.t
    kern = functools.partial(_moe_expert_kernel, f=f)
    last = n_blocks - 1
    grid_spec = pltpu.PrefetchScalarGridSpec(
        num_scalar_prefetch=2,
        grid=(n_blocks + 1,),
        in_specs=[
            pl.BlockSpec(memory_space=pl.ANY),
            pl.BlockSpec((rb, 1), lambda nb, be, nu: (jnp.minimum(nb, last), 0)),
            pl.BlockSpec((1, 1, d, f2), lambda nb, be, nu: (layer, be[nb], 0, 0)),
            pl.BlockSpec((1, 1, f, d), lambda nb, be, nu: (layer, be[nb], 0, 0)),
            pl.BlockSpec(memory_space=pl.ANY),
        ],
        out_specs=pl.BlockSpec((rb, half), lambda nb, be, nu: (jnp.minimum(nb, nu[0]), 0)),
        scratch_shapes=[
            pltpu.VMEM((3, rb, half), U32),
            pltpu.VMEM((d, f2), BF16),
            pltpu.VMEM((f, d), BF16),
            pltpu.SMEM((MOE_ID_RING * rb,), jnp.int32),
            pltpu.SemaphoreType.DMA((3,)),
            pltpu.SemaphoreType.DMA,
        ],
    )
    return pl.pallas_call(
        kern,
        grid_spec=grid_spec,
        out_shape=jax.ShapeDtypeStruct(((n_blocks + 1) * rb, half), U32),
        compiler_params=_params("arbitrary"),
        name="moe_experts",
    )(block_expert, n_used, tok, gate, w_in, w_out, hp)


def _moe_combine_kernel(pos_hbm, y_hbm, h_ref, wsi_ref, wso_ref, x_ref, gl_ref, gc_ref, o_ref,
                        ybuf, pos_smem, gsem, isem, *, lay, f, steps):
    rt = COMBINE_ROWS
    n_rows = TOP_K * rt
    i = pl.program_id(0)
    s = i % 2
    half = h_ref.shape[-1]

    def ring(step):
        return pl.multiple_of((step % COMBINE_POS_RING) * n_rows, n_rows)

    def pos_copy(step):
        return pltpu.make_async_copy(pos_hbm.at[step], pos_smem.at[pl.ds(ring(step), n_rows)], isem)

    def gather_row(k, r, slot):
        return pltpu.make_async_copy(y_hbm.at[pl.ds(pos_smem[k + r], 1)], ybuf.at[slot, pl.ds(r, 1)],
                                     gsem.at[slot])

    def gather_wait(slot):
        pltpu.make_async_copy(y_hbm.at[pl.ds(0, n_rows)], ybuf.at[slot], gsem.at[slot]).wait()

    @pl.when(i == 0)
    def _():
        for step in (0, 1):
            c = pos_copy(step)
            c.start()
            c.wait()

        def body(r, carry):
            gather_row(0, r, 0).start()
            return carry
        lax.fori_loop(0, n_rows, body, 0)

    @pl.when(i > 0)
    def _():
        pos_copy(i + 1).wait()

    pos_copy(i + 2).start()
    nk = ring(i + 1)
    gather_wait(s)
    lo = hi = None
    for k in range(TOP_K):
        for r in range(k * rt, (k + 1) * rt):
            gather_row(nk, r, 1 - s).start(priority=r % 2)
        l, h = _unpack_halves(ybuf[s, k * rt:(k + 1) * rt, :])
        lo, hi = (l, h) if lo is None else (lo + l, hi + h)
    hlo, hhi = _unpack_halves(h_ref[...])
    hu = (jnp.dot(hlo.astype(BF16), wsi_ref[:half, :], preferred_element_type=F32)
          + jnp.dot(hhi.astype(BF16), wsi_ref[half:, :], preferred_element_type=F32))
    act = (jax.nn.silu(hu[:, :f]) * hu[:, f:]).astype(BF16)
    shared = jnp.dot(act, wso_ref[...], preferred_element_type=F32)
    is_ctx = (pl.program_id(0) * COMBINE_ROWS) % lay.p < lay.ctx_len
    gate = jnp.where(is_ctx, gc_ref[0], gl_ref[0])
    o_ref[:, :half] = x_ref[:, :half] + gate[:, :half] * (lo + shared[:, :half])
    o_ref[:, half:] = x_ref[:, half:] + gate[:, half:] * (hi + shared[:, half:])

    @pl.when(i == steps - 1)
    def _():
        gather_wait(1 - s)
        pos_copy(i + 2).wait()


def _moe_combine(lay, y, pos, hp, ws_in, ws_out, x, mods, gate_idx):
    d = lay.d
    half = d // 2
    rt = COMBINE_ROWS
    assert lay.ctx_len % rt == 0 and lay.p % rt == 0
    steps = lay.t // rt
    assert pos.shape == (steps + 2, TOP_K * rt)
    f = ws_out.shape[0]
    return pl.pallas_call(
        functools.partial(_moe_combine_kernel, lay=lay, f=f, steps=steps),
        grid=(steps,),
        in_specs=[
            pl.BlockSpec(memory_space=pl.ANY),
            pl.BlockSpec(memory_space=pl.ANY),
            pl.BlockSpec((rt, half), lambda i: (i, 0)),
            pl.BlockSpec((d, 2 * f), lambda i: (0, 0)),
            pl.BlockSpec((f, d), lambda i: (0, 0)),
            pl.BlockSpec((rt, d), lambda i: (i, 0)),
            pl.BlockSpec((1, 1, d), lambda i: (i * rt // lay.p, 0, gate_idx)),
            pl.BlockSpec((1, 1, d), lambda i: (lay.batch, 0, gate_idx)),
        ],
        out_specs=pl.BlockSpec((rt, d), lambda i: (i, 0)),
        out_shape=jax.ShapeDtypeStruct((lay.t, d), F32),
        scratch_shapes=[
            pltpu.VMEM((2, TOP_K * rt, half), U32),
            pltpu.SMEM((COMBINE_POS_RING * TOP_K * rt,), jnp.int32),
            pltpu.SemaphoreType.DMA((2,)),
            pltpu.SemaphoreType.DMA,
        ],
        input_output_aliases={5: 0},
        compiler_params=_params("arbitrary"),
        name="moe_combine",
    )(pos, y, hp, ws_in, ws_out, x, mods, mods)


def _final_norm_kernel(x_ref, w_ref, o_ref):
    x = x_ref[0]
    o_ref[0] = x * lax.rsqrt(jnp.mean(x * x, axis=-1, keepdims=True) + EPS) * w_ref[...]


def _final_norm(lay, x, w):
    d = lay.d
    rows = _pick(lay.seq, (256, 128, 64))
    assert lay.ctx_len % rows == 0
    off = lay.ctx_len // rows
    return pl.pallas_call(
        _final_norm_kernel,
        grid=(lay.batch, lay.seq // rows),
        in_specs=[pl.BlockSpec((1, rows, d), lambda b, j: (b, off + j, 0)),
                  pl.BlockSpec((1, d), lambda b, j: (0, 0))],
        out_specs=pl.BlockSpec((1, rows, d), lambda b, j: (b, j, 0)),
        out_shape=jax.ShapeDtypeStruct((lay.batch, lay.seq, d), F32),
        compiler_params=_params("arbitrary", "arbitrary"),
        name="final_norm",
    )(x.reshape(lay.batch, lay.p, d), w)


def kernel(x, c, ctx, c_ctx, ada_w_down, ada_w_up, ada_b, norm_mix, norm_ffn, conv_w_in, conv_w, conv_w_out, gla_w_in, gla_w_a2_fwd, gla_b_a2_fwd, gla_w_a2_bwd, gla_b_a2_bwd, gla_norm, gla_w_out, router_w, router_bias, exp_w_in, exp_w_out, shared_w_in, shared_w_out, norm_final):
    batch, seq, d = x.shape
    ctx_len = ctx.shape[1]
    depth = ada_w_down.shape[0]
    lay = _Layout(batch, ctx_len, seq, d)
    qk = d // 2
    tn = _pick(d, (1024, 512, 256, 128))

    cond = jnp.zeros((MOD_ROWS, d), F32).at[:batch].set(c).at[batch].set(c_ctx)
    mods_all = _ada_all(cond, ada_w_down, ada_w_up, ada_b)
    xs = jnp.concatenate([ctx, x], axis=1).reshape(lay.t, d)

    for i in range(depth):
        j = i // 2
        mods = mods_all[i].reshape(MOD_ROWS, 1, N_MOD * d)
        nw_mix = norm_mix[i].reshape(1, d)
        if i % 2 == 0:
            y = _norm_proj(lay, xs, nw_mix, mods, 0, 1, conv_w_in[j].astype(BF16), out_dtype=BF16,
                           tn=_pick(d, (256, 128)), conv_w=conv_w[j], name="conv_in")
            xs = _out_proj(lay, y, conv_w_out[j].astype(BF16), xs, mods, 2, tn=tn, name="conv_out")
        else:
            n_main = 2 * qk + 2 * d
            w_aux = jnp.zeros((d, LANES), BF16).at[:, :2 * GLA_GATE_RANK].set(
                gla_w_in[j][:, n_main:].astype(BF16))
            qkvr, a = _norm_proj(lay, xs, nw_mix, mods, 0, 1, gla_w_in[j].astype(BF16), out_dtype=BF16,
                                 tn=tn, n=n_main, w_aux=w_aux, name="gla_in")
            waf = jnp.zeros((LANES, qk), BF16).at[:GLA_GATE_RANK].set(gla_w_a2_fwd[j].astype(BF16))
            wab = jnp.zeros((LANES, qk), BF16).at[GLA_GATE_RANK:2 * GLA_GATE_RANK].set(
                gla_w_a2_bwd[j].astype(BF16))
            y = _gla(lay, qkvr, a, waf, gla_b_a2_fwd[j].reshape(1, qk), wab,
                     gla_b_a2_bwd[j].reshape(1, qk), gla_norm[j].reshape(1, d // GLA_HEADS))
            xs = _out_proj(lay, y, gla_w_out[j].astype(BF16), xs, mods, 2, tn=tn, name="gla_out")

        rw = jnp.zeros((d, LANES), BF16).at[:, :N_EXPERTS].set(router_w[i].astype(BF16))
        rb = jnp.zeros((1, LANES), F32).at[0, :N_EXPERTS].set(router_bias[i])
        hp, gates, ids = _router(lay, xs, norm_ffn[i].reshape(1, d), mods, rw, rb)
        tok, gate, block_expert, n_used, pos = _moe_plan(lay, gates, ids)
        y = _moe_experts(lay, i, hp, tok, gate, block_expert, n_used, exp_w_in, exp_w_out)
        xs = _moe_combine(lay, y, pos, hp, shared_w_in[i].astype(BF16), shared_w_out[i].astype(BF16),
                          xs, mods, 5)

    return _final_norm(lay, xs, norm_final.reshape(1, d))
```

```python
import functools

import jax
import jax.numpy as jnp
import numpy as np
from jax import lax
from jax.experimental import pallas as pl
from jax.experimental.pallas import tpu as pltpu

GRID_W = 64
EPS = 1e-6
N_MOD = 6
GLA_HEADS = 8
GLA_GATE_RANK = 16
GLA_GATE_TAU = 16.0
GLA_LOG_DECAY_FLOOR = -1.0
N_EXPERTS = 64
N_GROUPS = 8
TOPK_GROUPS = 4
TOP_K = 8
ROUTED_SCALE = 2.5

LANES = 128
GLA_ROWS = 128
TILES_PER_SAMPLE = 4
MOD_ROWS = 16
VMEM_LIMIT_BYTES = 56 * 1024 * 1024

MOE_ROWS = 256
COMBINE_ROWS = 128
MOE_EXTRA_BLOCKS = 5
MOE_ID_RING = 8
COMBINE_POS_RING = 4
COMBINE_SPARE_GROUPS = 3

BF16 = jnp.bfloat16
F32 = jnp.float32
U32 = jnp.uint32
HIGH_HALF = np.uint32(0xFFFF0000)
NEG_INF = float("-inf")


def _params(*sem):
    return pltpu.CompilerParams(dimension_semantics=sem, vmem_limit_bytes=VMEM_LIMIT_BYTES)


def _pick(n, prefs):
    for p in prefs:
        if n % p == 0:
            return p
    return n


def _ada_kernel(cond_ref, wd_ref, wu_ref, b_ref, o_ref, z_ref):
    @pl.when(pl.program_id(1) == 0)
    def _():
        s = jax.nn.silu(cond_ref[...]).astype(BF16)
        z_ref[...] = jnp.dot(s, wd_ref[0].astype(BF16), preferred_element_type=F32)

    o_ref[0] = jnp.dot(z_ref[...].astype(BF16), wu_ref[0].astype(BF16),
                       preferred_element_type=F32) + b_ref[0]


def _ada_all(cond, w_down, w_up, b):
    depth, d, r = w_down.shape
    n = w_up.shape[-1]
    tn = _pick(n, (2048, 1024, 512, 256, 128))
    return pl.pallas_call(
        _ada_kernel,
        grid=(depth, n // tn),
        in_specs=[
            pl.BlockSpec((MOD_ROWS, d), lambda l, j: (0, 0)),
            pl.BlockSpec((1, d, r), lambda l, j: (l, 0, 0)),
            pl.BlockSpec((1, r, tn), lambda l, j: (l, 0, j)),
            pl.BlockSpec((1, 1, tn), lambda l, j: (l, 0, j)),
        ],
        out_specs=pl.BlockSpec((1, MOD_ROWS, tn), lambda l, j: (l, 0, j)),
        out_shape=jax.ShapeDtypeStruct((depth, MOD_ROWS, n), F32),
        scratch_shapes=[pltpu.VMEM((MOD_ROWS, r), F32)],
        compiler_params=_params("arbitrary", "arbitrary"),
        name="ada_mod",
    )(cond, w_down, w_up, b.reshape(depth, 1, n))


class _Layout:
    def __init__(self, batch, ctx_len, seq, d):
        self.batch, self.ctx_len, self.seq, self.d = batch, ctx_len, seq, d
        self.p = ctx_len + seq
        self.t = batch * self.p
        assert self.p % TILES_PER_SAMPLE == 0
        self.tm = self.p // TILES_PER_SAMPLE
        assert GRID_W & (GRID_W - 1) == 0
        assert self.tm % GRID_W == 0 and ctx_len % GRID_W == 0 and self.tm % 16 == 0
        assert ctx_len <= self.tm and batch < MOD_ROWS
        self.n_tiles = self.t // self.tm
        self.slab = _pick(self.tm, (32, 16))
        assert ctx_len % self.slab == 0


def _mod_specs(lay, idx, two_axes=True):
    d = lay.d
    if two_axes:
        lat = pl.BlockSpec((1, 1, d), lambda i, j: (i // TILES_PER_SAMPLE, 0, idx))
        ctx = pl.BlockSpec((1, 1, d), lambda i, j: (lay.batch, 0, idx))
    else:
        lat = pl.BlockSpec((1, 1, d), lambda i: (i // TILES_PER_SAMPLE, 0, idx))
        ctx = pl.BlockSpec((1, 1, d), lambda i: (lay.batch, 0, idx))
    return lat, ctx


def _tile_row0(lay):
    return (pl.program_id(0) % TILES_PER_SAMPLE) * lay.tm


def _row_in_sample(lay, shape):
    return _tile_row0(lay) + lax.broadcasted_iota(jnp.int32, shape, 0)


def _pack_halves(x):
    half = x.shape[1] // 2
    xb = x.astype(BF16).astype(F32)
    lo = lax.bitcast_convert_type(xb[:, :half], U32) >> 16
    hi = lax.bitcast_convert_type(xb[:, half:], U32) & HIGH_HALF
    return hi | lo


def _unpack_halves(w):
    lo = lax.bitcast_convert_type(w << 16, F32)
    hi = lax.bitcast_convert_type(w & HIGH_HALF, F32)
    return lo, hi


def _modnorm_rows(lay, x_ref, nw_ref, shl_ref, scl_ref, shc_ref, scc_ref, h_ref, packed_ref=None):
    nw = nw_ref[...]
    slab = lay.slab
    row0 = _tile_row0(lay)

    def body(s, carry):
        r = pl.multiple_of(s * slab, slab)
        x = x_ref[pl.ds(r, slab), :]
        ms = jnp.mean(x * x, axis=-1, keepdims=True)
        y = x * lax.rsqrt(ms + EPS) * nw
        is_ctx = row0 + r < lay.ctx_len
        sc = jnp.where(is_ctx, scc_ref[0], scl_ref[0])
        sh = jnp.where(is_ctx, shc_ref[0], shl_ref[0])
        h = y * (1.0 + sc) + sh
        h_ref[pl.ds(r, slab), :] = h.astype(h_ref.dtype)
        if packed_ref is not None:
            packed_ref[pl.ds(r, slab), :] = _pack_halves(h)
        return carry

    lax.fori_loop(0, lay.tm // slab, body, 0)


def _proj_kernel(x_ref, nw_ref, shl, scl, shc, scc, w_ref, o_ref, h_ref, *, lay):
    @pl.when(pl.program_id(1) == 0)
    def _():
        _modnorm_rows(lay, x_ref, nw_ref, shl, scl, shc, scc, h_ref)

    o_ref[...] = jnp.dot(h_ref[...], w_ref[...], preferred_element_type=F32).astype(o_ref.dtype)


def _proj_aux_kernel(x_ref, nw_ref, shl, scl, shc, scc, w_ref, wa_ref, o_ref, a_ref, h_ref, *, lay):
    @pl.when(pl.program_id(1) == 0)
    def _():
        _modnorm_rows(lay, x_ref, nw_ref, shl, scl, shc, scc, h_ref)
        a_ref[...] = jnp.dot(h_ref[...], wa_ref[...], preferred_element_type=F32)

    o_ref[...] = jnp.dot(h_ref[...], w_ref[...], preferred_element_type=F32).astype(o_ref.dtype)


def _conv_proj_kernel(x_ref, nw_ref, shl, scl, shc, scc, wb_ref, wc_ref, wv_ref, cw_ref, o_ref, h_ref,
                      *, lay, tc):
    @pl.when(pl.program_id(1) == 0)
    def _():
        _modnorm_rows(lay, x_ref, nw_ref, shl, scl, shc, scc, h_ref)

    tm = lay.tm
    h = h_ref[...]
    bg = jnp.dot(h, wb_ref[...], preferred_element_type=F32)
    cg = jnp.dot(h, wc_ref[...], preferred_element_type=F32)
    v = jnp.dot(h, wv_ref[...], preferred_element_type=F32)
    u = cg * v
    row = _row_in_sample(lay, (tm, tc))
    is_ctx = row < lay.ctx_len
    col = (row - lay.ctx_len) & (GRID_W - 1)
    first = (is_ctx & (row == 0)) | (~is_ctx & (col == 0))
    last = (is_ctx & (row == lay.ctx_len - 1)) | (~is_ctx & (col == GRID_W - 1))
    prev = jnp.where(first, 0.0, pltpu.roll(u, 1, axis=0))
    nxt = jnp.where(last, 0.0, pltpu.roll(u, tm - 1, axis=0))
    cw = cw_ref[...]
    conv = prev * cw[0:1, :] + u * cw[1:2, :] + nxt * cw[2:3, :]
    o_ref[...] = (bg * conv).astype(o_ref.dtype)


def _norm_proj(lay, x, nw, mods, shift_idx, scale_idx, w, *, out_dtype, tn, n=None, w_aux=None,
               conv_w=None, name):
    d = w.shape[0]
    n = w.shape[1] if n is None else n
    assert n % tn == 0
    shl, shc = _mod_specs(lay, shift_idx)
    scl, scc = _mod_specs(lay, scale_idx)
    in_specs = [
        pl.BlockSpec((lay.tm, d), lambda i, j: (i, 0)),
        pl.BlockSpec((1, d), lambda i, j: (0, 0)),
        shl, scl, shc, scc,
        pl.BlockSpec((d, tn), lambda i, j: (0, j)),
    ]
    args = [x, nw, mods, mods, mods, mods, w]
    scratch = [pltpu.VMEM((lay.tm, d), BF16)]
    params = _params("arbitrary", "arbitrary")
    if conv_w is not None:
        tc = tn
        nct = n // 3 // tc
        in_specs[-1:] = [pl.BlockSpec((d, tc), functools.partial(lambda i, j, k: (0, k * nct + j), k=k))
                         for k in range(3)]
        in_specs.append(pl.BlockSpec((3, tc), lambda i, j: (0, j)))
        return pl.pallas_call(
            functools.partial(_conv_proj_kernel, lay=lay, tc=tc),
            grid=(lay.n_tiles, nct), in_specs=in_specs,
            out_specs=pl.BlockSpec((lay.tm, tc), lambda i, j: (i, j)),
            out_shape=jax.ShapeDtypeStruct((lay.t, n // 3), out_dtype),
            scratch_shapes=scratch, compiler_params=params, name=name,
        )(*args[:-1], w, w, w, conv_w)
    if w_aux is not None:
        na = w_aux.shape[1]
        in_specs.append(pl.BlockSpec((d, na), lambda i, j: (0, 0)))
        return pl.pallas_call(
            functools.partial(_proj_aux_kernel, lay=lay),
            grid=(lay.n_tiles, n // tn), in_specs=in_specs,
            out_specs=[pl.BlockSpec((lay.tm, tn), lambda i, j: (i, j)),
                       pl.BlockSpec((lay.tm, na), lambda i, j: (i, 0))],
            out_shape=[jax.ShapeDtypeStruct((lay.t, n), out_dtype),
                       jax.ShapeDtypeStruct((lay.t, na), F32)],
            scratch_shapes=scratch, compiler_params=params, name=name,
        )(*args, w_aux)
    return pl.pallas_call(
        functools.partial(_proj_kernel, lay=lay),
        grid=(lay.n_tiles, n // tn), in_specs=in_specs,
        out_specs=pl.BlockSpec((lay.tm, tn), lambda i, j: (i, j)),
        out_shape=jax.ShapeDtypeStruct((lay.t, n), out_dtype),
        scratch_shapes=scratch, compiler_params=params, name=name,
    )(*args)


def _out_proj_kernel(a_ref, w_ref, x_ref, gl_ref, gc_ref, o_ref, *, lay):
    y = jnp.dot(a_ref[...], w_ref[...], preferred_element_type=F32)
    is_ctx = _row_in_sample(lay, y.shape) < lay.ctx_len
    gate = jnp.where(is_ctx, gc_ref[0], gl_ref[0])
    o_ref[...] = x_ref[...] + gate * y


def _out_proj(lay, a, w, x, mods, gate_idx, *, tn, name):
    k, d = w.shape
    gl = pl.BlockSpec((1, 1, tn), lambda i, j: (i // TILES_PER_SAMPLE, 0, gate_idx * (d // tn) + j))
    gc = pl.BlockSpec((1, 1, tn), lambda i, j: (lay.batch, 0, gate_idx * (d // tn) + j))
    return pl.pallas_call(
        functools.partial(_out_proj_kernel, lay=lay),
        grid=(lay.n_tiles, d // tn),
        in_specs=[
            pl.BlockSpec((lay.tm, k), lambda i, j: (i, 0)),
            pl.BlockSpec((k, tn), lambda i, j: (0, j)),
            pl.BlockSpec((lay.tm, tn), lambda i, j: (i, j)),
            gl, gc,
        ],
        out_specs=pl.BlockSpec((lay.tm, tn), lambda i, j: (i, j)),
        out_shape=jax.ShapeDtypeStruct((lay.t, d), F32),
        input_output_aliases={2: 0},
        compiler_params=_params("arbitrary", "arbitrary"),
        name=name,
    )(a, w, x, mods, mods)


def _gla_kernel(q_ref, k_ref, v_ref, r_ref, a_ref, waf_ref, baf_ref, wab_ref, bab_ref, nw_ref,
                y_ref, of_ref, ob_ref, sf_ref, sb_ref, *, n_chunks, n_ctx_chunks, dk):
    c_rows = GLA_ROWS
    scale = dk ** -0.5
    ri = lax.broadcasted_iota(jnp.int32, (c_rows, c_rows), 0)
    ci = lax.broadcasted_iota(jnp.int32, (c_rows, c_rows), 1)
    nt = (((1,), (1,)), ((), ()))
    tn_dims = (((0,), (0,)), ((), ()))

    def chunk(c, forward, s_ref):
        rows = pl.ds(pl.multiple_of(c * c_rows, c_rows), c_rows)
        keep = (ci <= ri) if forward else (ci >= ri)
        tri = keep.astype(BF16)
        wa, ba = (waf_ref, baf_ref) if forward else (wab_ref, bab_ref)
        z = jnp.dot(a_ref[rows, :].astype(BF16), wa[...], preferred_element_type=F32) + ba[...]
        g = jnp.maximum(jax.nn.log_sigmoid(z) / GLA_GATE_TAU, GLA_LOG_DECAY_FLOOR)
        g1 = g.astype(BF16)
        rem = g - g1.astype(F32)
        g2 = rem.astype(BF16)
        g3 = (rem - g2.astype(F32)).astype(BF16)
        bcum = (jnp.dot(tri, g1, preferred_element_type=F32)
                + jnp.dot(tri, g2, preferred_element_type=F32)
                + jnp.dot(tri, g3, preferred_element_type=F32))
        mid = bcum[c_rows // 2:c_rows // 2 + 1, :]
        tot = bcum[c_rows - 1:c_rows, :] if forward else bcum[0:1, :]
        q = q_ref[rows, :].astype(F32) * scale
        k = k_ref[rows, :].astype(F32)
        v = v_ref[rows, :]
        q_rel = (q * jnp.exp(bcum - mid)).astype(BF16)
        k_rel = (k * jnp.exp(mid - bcum)).astype(BF16)
        q_dec = (q * jnp.exp(bcum)).astype(BF16)
        k_end = (k * jnp.exp(tot - bcum)).astype(BF16)
        att = lax.dot_general(q_rel, k_rel, nt, preferred_element_type=F32)
        att = jnp.where(keep, att, 0.0).astype(BF16)
        s = s_ref[...]
        o = (jnp.dot(att, v, preferred_element_type=F32)
             + lax.dot_general(q_dec, s.astype(BF16), nt, preferred_element_type=F32))
        s_ref[...] = jnp.exp(tot) * s + lax.dot_general(v, k_end, tn_dims, preferred_element_type=F32)
        return rows, o

    sf_ref[...] = jnp.zeros_like(sf_ref)
    sb_ref[...] = jnp.zeros_like(sb_ref)

    def scan_body(n, carry):
        rows, o = chunk(n, True, sf_ref)
        of_ref[rows, :] = o
        c = jnp.where(n < n_ctx_chunks, n_ctx_chunks - 1 - n, n_chunks - 1 - (n - n_ctx_chunks))
        rows, o = chunk(c, False, sb_ref)
        ob_ref[rows, :] = o
        return carry

    lax.fori_loop(0, n_chunks, scan_body, 0)
    nw = nw_ref[...]

    def out_body(n, carry):
        rows = pl.ds(pl.multiple_of(n * c_rows, c_rows), c_rows)
        o = of_ref[rows, :] + ob_ref[rows, :]
        o = o * lax.rsqrt(jnp.mean(o * o, axis=-1, keepdims=True) + EPS) * nw
        y_ref[rows, :] = (o * jax.nn.silu(r_ref[rows, :].astype(F32))).astype(y_ref.dtype)
        return carry

    lax.fori_loop(0, n_chunks, out_body, 0)


def _gla(lay, qkvr, a, waf, baf, wab, bab, norm_w):
    d = lay.d
    dk = d // 2 // GLA_HEADS
    dv = d // GLA_HEADS
    assert lay.ctx_len % GLA_ROWS == 0 and lay.seq % GLA_ROWS == 0
    n_chunks = lay.p // GLA_ROWS
    kern = functools.partial(_gla_kernel, n_chunks=n_chunks,
                             n_ctx_chunks=lay.ctx_len // GLA_ROWS, dk=dk)
    h = GLA_HEADS
    return pl.pallas_call(
        kern,
        grid=(lay.batch, h),
        in_specs=[
            pl.BlockSpec((lay.p, dk), lambda b, i: (b, i)),
            pl.BlockSpec((lay.p, dk), lambda b, i: (b, h + i)),
            pl.BlockSpec((lay.p, dv), lambda b, i: (b, h + i)),
            pl.BlockSpec((lay.p, dv), lambda b, i: (b, 2 * h + i)),
            pl.BlockSpec((lay.p, LANES), lambda b, i: (b, 0)),
            pl.BlockSpec((LANES, dk), lambda b, i: (0, i)),
            pl.BlockSpec((1, dk), lambda b, i: (0, i)),
            pl.BlockSpec((LANES, dk), lambda b, i: (0, i)),
            pl.BlockSpec((1, dk), lambda b, i: (0, i)),
            pl.BlockSpec((1, dv), lambda b, i: (0, 0)),
        ],
        out_specs=pl.BlockSpec((lay.p, dv), lambda b, i: (b, i)),
        out_shape=jax.ShapeDtypeStruct((lay.t, d), BF16),
        scratch_shapes=[pltpu.VMEM((lay.p, dv), F32), pltpu.VMEM((lay.p, dv), F32),
                        pltpu.VMEM((dv, dk), F32), pltpu.VMEM((dv, dk), F32)],
        compiler_params=_params("arbitrary", "arbitrary"),
        name="gla_scan",
    )(qkvr, qkvr, qkvr, qkvr, a, waf, baf, wab, bab, norm_w)


def _route(logits, bias):
    n_rows = logits.shape[0]
    per_group = N_EXPERTS // N_GROUPS
    lane_i = lax.broadcasted_iota(jnp.int32, (n_rows, LANES), 1)
    lane = lane_i.astype(F32)
    group = (lane_i // per_group).astype(F32)
    valid = lane_i < N_EXPERTS
    big = float(2 * LANES)
    scores = jax.nn.sigmoid(logits)
    sel = jnp.where(valid, scores + bias, NEG_INF)

    def partner(x, s):
        return jnp.where((lane_i & s) == 0, pltpu.roll(x, LANES - s, axis=1), pltpu.roll(x, s, axis=1))

    def group_reduce(x, op):
        s = 1
        while s < per_group:
            x = op(x, partner(x, s))
            s *= 2
        return x

    m1 = group_reduce(sel, jnp.maximum)
    first = group_reduce(jnp.where(sel == m1, lane, big), jnp.minimum)
    m2 = group_reduce(jnp.where(lane == first, NEG_INF, sel), jnp.maximum)
    cur = jnp.where(valid, m1 + m2, NEG_INF)

    def pick_best(cur, ident):
        m = jnp.max(cur, axis=1, keepdims=True)
        best = jnp.min(jnp.where(cur == m, ident, big), axis=1, keepdims=True)
        return ident == best

    group_ok = jnp.zeros((n_rows, LANES), jnp.bool_)
    for _ in range(TOPK_GROUPS):
        p = pick_best(cur, group)
        group_ok = group_ok | p
        cur = jnp.where(p, NEG_INF, cur)

    cur = jnp.where(group_ok & valid, sel, NEG_INF)
    picked = jnp.zeros((n_rows, LANES), jnp.bool_)
    ids = jnp.zeros((n_rows, LANES), F32)
    for it in range(TOP_K):
        m = jnp.max(cur, axis=1, keepdims=True)
        best = jnp.min(jnp.where(cur == m, lane, big), axis=1, keepdims=True)
        p = lane == best
        picked = picked | p
        cur = jnp.where(p, NEG_INF, cur)
        ids = jnp.where(lane_i == it, best, ids)

    w = jnp.where(picked, scores, 0.0)
    gates = w / jnp.sum(w, axis=1, keepdims=True) * ROUTED_SCALE
    return gates, ids.astype(jnp.int32)


def _router_kernel(x_ref, nw_ref, shl, scl, shc, scc, rw_ref, rb_ref, hp_ref, g_ref, id_ref, h_ref, *, lay):
    _modnorm_rows(lay, x_ref, nw_ref, shl, scl, shc, scc, h_ref, packed_ref=hp_ref)
    logits = jnp.dot(h_ref[...], rw_ref[...], preferred_element_type=F32)
    g_ref[...], id_ref[...] = _route(logits, rb_ref[...])


def _router(lay, x, nw, mods, rw, rb):
    d = lay.d
    shl, shc = _mod_specs(lay, 3, two_axes=False)
    scl, scc = _mod_specs(lay, 4, two_axes=False)
    return pl.pallas_call(
        functools.partial(_router_kernel, lay=lay),
        grid=(lay.n_tiles,),
        in_specs=[
            pl.BlockSpec((lay.tm, d), lambda i: (i, 0)),
            pl.BlockSpec((1, d), lambda i: (0, 0)),
            shl, scl, shc, scc,
            pl.BlockSpec((d, LANES), lambda i: (0, 0)),
            pl.BlockSpec((1, LANES), lambda i: (0, 0)),
        ],
        out_specs=[pl.BlockSpec((lay.tm, d // 2), lambda i: (i, 0)),
                   pl.BlockSpec((lay.tm, LANES), lambda i: (i, 0)),
                   pl.BlockSpec((lay.tm, LANES), lambda i: (i, 0))],
        out_shape=[jax.ShapeDtypeStruct((lay.t, d // 2), U32),
                   jax.ShapeDtypeStruct((lay.t, LANES), F32),
                   jax.ShapeDtypeStruct((lay.t, LANES), jnp.int32)],
        scratch_shapes=[pltpu.VMEM((lay.tm, d), BF16)],
        compiler_params=_params("arbitrary"),
        name="moe_router",
    )(x, nw, mods, mods, mods, mods, rw, rb)


def _moe_plan(lay, gates, ids):
    t, rb = lay.t, MOE_ROWS
    n_pairs = t * TOP_K
    n_blocks = n_pairs // rb + N_EXPERTS
    assert N_EXPERTS * n_pairs < 2 ** 31
    eidx = ids[:, :TOP_K]
    w = jnp.take_along_axis(gates, eidx, axis=1).reshape(-1)
    key = eidx.reshape(-1) * n_pairs + jnp.arange(n_pairs, dtype=jnp.int32)
    key_s, w_s = lax.sort((key, w), num_keys=1)
    bounds = jnp.searchsorted(key_s, jnp.arange(N_EXPERTS + 1, dtype=jnp.int32) * n_pairs).astype(jnp.int32)
    start, counts = bounds[:-1], bounds[1:] - bounds[:-1]
    padded = (counts + rb - 1) // rb * rb
    pad_end = jnp.cumsum(padded)
    pad_start = pad_end - padded
    blk = jnp.arange(n_blocks + MOE_EXTRA_BLOCKS, dtype=jnp.int32) - 1
    e_blk = jnp.clip(jnp.searchsorted(pad_end, blk * rb, side="right", method="compare_all"),
                     0, N_EXPERTS - 1)
    r = jnp.arange(rb, dtype=jnp.int32)[None, :]
    in_run = (blk * rb - pad_start[e_blk])[:, None] + r
    valid = ((blk >= 0) & (blk * rb < pad_end[-1]))[:, None] & (in_run < counts[e_blk][:, None])
    src = jnp.clip(start[e_blk][:, None] + in_run, 0, n_pairs - 1)
    pair = key_s[src] % n_pairs
    tok = jnp.where(valid, pair // TOP_K, r)
    gate = jnp.where(valid, w_s[src], 0.0)[1:n_blocks + 1].reshape(n_blocks * rb, 1)
    n_used = pad_end[-1] // rb
    step_blk = jnp.minimum(jnp.arange(n_blocks + 1, dtype=jnp.int32), n_used - 1)
    block_expert = jnp.searchsorted(pad_end, step_blk * rb, side="right",
                                    method="compare_all").astype(jnp.int32)
    chose = jnp.sum(eidx[:, :, None] == jnp.arange(N_EXPERTS, dtype=jnp.int32)[None, None, :], axis=1,
                    dtype=jnp.int32)
    earlier = jnp.cumsum(chose, axis=0) - chose
    pos = pad_start[eidx] + jnp.take_along_axis(earlier, eidx, axis=1)
    rt = COMBINE_ROWS
    pos = pos.reshape(t // rt, rt, TOP_K).transpose(0, 2, 1).reshape(t // rt, TOP_K * rt)
    pos = jnp.concatenate([pos, jnp.zeros((COMBINE_SPARE_GROUPS, TOP_K * rt), pos.dtype)]).astype(jnp.int32)
    return tok.astype(jnp.int32), gate, block_expert, n_used.reshape(1).astype(jnp.int32), pos


def _moe_expert_kernel(be_ref, nu_ref, tok_hbm, g_ref, win_ref, wout_ref, h_hbm, y_ref,
                       hbuf, winb, woutb, tok_smem, gsem, isem, *, f):
    nb = pl.program_id(0)
    pl.when(nb <= nu_ref[0])(functools.partial(
        _moe_expert_step, nb, nu_ref[0], be_ref, tok_hbm, g_ref, win_ref, wout_ref, h_hbm, y_ref,
        hbuf, winb, woutb, tok_smem, gsem, isem, f=f))


def _moe_expert_step(nb, n_used, be_ref, tok_hbm, g_ref, win_ref, wout_ref, h_hbm, y_ref,
                     hbuf, winb, woutb, tok_smem, gsem, isem, *, f):
    rb = MOE_ROWS
    hs = nb % 3
    half = hbuf.shape[-1]

    def ring(block):
        return pl.multiple_of(((block + 1) % MOE_ID_RING) * rb, rb)

    def ids_copy(block):
        return pltpu.make_async_copy(tok_hbm.at[block + 1], tok_smem.at[pl.ds(ring(block), rb)], isem)

    def gather_row(k, r, s):
        return pltpu.make_async_copy(h_hbm.at[pl.ds(tok_smem[k + r], 1)], hbuf.at[s, pl.ds(r, 1)],
                                     gsem.at[s])

    def gather_wait(s):
        pltpu.make_async_copy(h_hbm.at[pl.ds(0, rb)], hbuf.at[s], gsem.at[s]).wait()

    @pl.when(nb == 0)
    def _():
        for block in (0, 1, 2):
            c = ids_copy(block)
            c.start()
            c.wait()
        for block in (0, 1):
            def body(r, carry, block=block):
                gather_row((block + 1) * rb, r, block).start()
                return carry
            lax.fori_loop(0, rb, body, 0)

    @pl.when(nb > 0)
    def _():
        ids_copy(nb + 2).wait()

    ids_copy(nb + 3).start()

    @pl.when((nb == 0) | (be_ref[nb] != be_ref[jnp.maximum(nb - 1, 0)]))
    def _():
        winb[...] = win_ref[0, 0].astype(BF16)
        woutb[...] = wout_ref[0, 0].astype(BF16)

    gk = ring(nb + 2)
    ghs = (nb + 2) % 3
    n_groups = 8
    per_group = rb // n_groups

    def issue(group):
        for r in range(group * per_group, (group + 1) * per_group):
            gather_row(gk, r, ghs).start(priority=r % 2)

    gather_wait(hs)
    issue(0)
    cols = half // 4
    hu = jnp.zeros((rb, 2 * f), F32)
    for piece in range(4):
        c0 = piece * cols
        lo, hi = _unpack_halves(hbuf[hs, :, c0:c0 + cols])
        hu = (hu + jnp.dot(lo.astype(BF16), winb[c0:c0 + cols, :], preferred_element_type=F32)
              + jnp.dot(hi.astype(BF16), winb[half + c0:half + c0 + cols, :],
                        preferred_element_type=F32))
        issue(1 + piece)
    act = (jax.nn.silu(hu[:, :f]) * hu[:, f:] * g_ref[...]).astype(BF16)
    for piece in range(4):
        c0 = piece * cols
        y_lo = jnp.dot(act, woutb[:, c0:c0 + cols], preferred_element_type=F32)
        y_hi = jnp.dot(act, woutb[:, half + c0:half + c0 + cols], preferred_element_type=F32)
        y_ref[:, c0:c0 + cols] = _pack_halves(jnp.concatenate([y_lo, y_hi], axis=1))
        if piece < 3:
            issue(5 + piece)

    @pl.when(nb == n_used)
    def _():
        gather_wait((nb + 1) % 3)
        gather_wait((nb + 2) % 3)
        ids_copy(nb + 3).wait()


def _moe_experts(lay, layer, hp, tok, gate, block_expert, n_used, w_in, w_out):
    _, ne, d, f2 = w_in.shape
    f = f2 // 2
    rb = MOE_ROWS
    half = d // 2
    n_blocks = gate.shape[0] // rb
    assert tok.shape == (n_blocks + MOE_EXTRA_BLOCKS, rb) and rb <= lay.t
    kern = functools.partial(_moe_expert_kernel, f=f)
    last = n_blocks - 1
    grid_spec = pltpu.PrefetchScalarGridSpec(
        num_scalar_prefetch=2,
        grid=(n_blocks + 1,),
        in_specs=[
            pl.BlockSpec(memory_space=pl.ANY),
            pl.BlockSpec((rb, 1), lambda nb, be, nu: (jnp.minimum(nb, last), 0)),
            pl.BlockSpec((1, 1, d, f2), lambda nb, be, nu: (layer, be[nb], 0, 0)),
            pl.BlockSpec((1, 1, f, d), lambda nb, be, nu: (layer, be[nb], 0, 0)),
            pl.BlockSpec(memory_space=pl.ANY),
        ],
        out_specs=pl.BlockSpec((rb, half), lambda nb, be, nu: (jnp.minimum(nb, nu[0]), 0)),
        scratch_shapes=[
            pltpu.VMEM((3, rb, half), U32),
            pltpu.VMEM((d, f2), BF16),
            pltpu.VMEM((f, d), BF16),
            pltpu.SMEM((MOE_ID_RING * rb,), jnp.int32),
            pltpu.SemaphoreType.DMA((3,)),
            pltpu.SemaphoreType.DMA,
        ],
    )
    return pl.pallas_call(
        kern,
        grid_spec=grid_spec,
        out_shape=jax.ShapeDtypeStruct(((n_blocks + 1) * rb, half), U32),
        compiler_params=_params("arbitrary"),
        name="moe_experts",
    )(block_expert, n_used, tok, gate, w_in, w_out, hp)


def _moe_combine_kernel(pos_hbm, y_hbm, h_ref, wsi_ref, wso_ref, x_ref, gl_ref, gc_ref, o_ref,
                        ybuf, pos_smem, gsem, isem, *, lay, f, steps):
    rt = COMBINE_ROWS
    n_rows = TOP_K * rt
    i = pl.program_id(0)
    s = i % 3
    half = h_ref.shape[-1]

    def ring(step):
        return pl.multiple_of((step % COMBINE_POS_RING) * n_rows, n_rows)

    def pos_copy(step):
        return pltpu.make_async_copy(pos_hbm.at[step], pos_smem.at[pl.ds(ring(step), n_rows)], isem)

    def gather_row(k, r, slot):
        return pltpu.make_async_copy(y_hbm.at[pl.ds(pos_smem[k + r], 1)], ybuf.at[slot, pl.ds(r, 1)],
                                     gsem.at[slot])

    def gather_wait(slot):
        pltpu.make_async_copy(y_hbm.at[pl.ds(0, n_rows)], ybuf.at[slot], gsem.at[slot]).wait()

    @pl.when(i == 0)
    def _():
        for step in (0, 1, 2):
            c = pos_copy(step)
            c.start()
            c.wait()
        for step in (0, 1):
            def body(r, carry, step=step):
                gather_row(step * n_rows, r, step).start()
                return carry
            lax.fori_loop(0, n_rows, body, 0)

    @pl.when(i > 0)
    def _():
        pos_copy(i + 2).wait()

    pos_copy(i + 3).start()
    nk = ring(i + 2)
    ns = (i + 2) % 3
    gather_wait(s)
    lo = hi = None
    for k in range(TOP_K):
        for r in range(k * rt, (k + 1) * rt):
            gather_row(nk, r, ns).start(priority=r % 2)
        l, h = _unpack_halves(ybuf[s, k * rt:(k + 1) * rt, :])
        lo, hi = (l, h) if lo is None else (lo + l, hi + h)
    hlo, hhi = _unpack_halves(h_ref[...])
    hu = (jnp.dot(hlo.astype(BF16), wsi_ref[:half, :], preferred_element_type=F32)
          + jnp.dot(hhi.astype(BF16), wsi_ref[half:, :], preferred_element_type=F32))
    act = (jax.nn.silu(hu[:, :f]) * hu[:, f:]).astype(BF16)
    shared = jnp.dot(act, wso_ref[...], preferred_element_type=F32)
    is_ctx = (pl.program_id(0) * COMBINE_ROWS) % lay.p < lay.ctx_len
    gate = jnp.where(is_ctx, gc_ref[0], gl_ref[0])
    o_ref[:, :half] = x_ref[:, :half] + gate[:, :half] * (lo + shared[:, :half])
    o_ref[:, half:] = x_ref[:, half:] + gate[:, half:] * (hi + shared[:, half:])

    @pl.when(i == steps - 1)
    def _():
        gather_wait((i + 1) % 3)
        gather_wait(ns)
        pos_copy(i + 3).wait()


def _moe_combine(lay, y, pos, hp, ws_in, ws_out, x, mods, gate_idx):
    d = lay.d
    half = d // 2
    rt = COMBINE_ROWS
    assert lay.ctx_len % rt == 0 and lay.p % rt == 0
    steps = lay.t // rt
    assert pos.shape == (steps + COMBINE_SPARE_GROUPS, TOP_K * rt)
    f = ws_out.shape[0]
    return pl.pallas_call(
        functools.partial(_moe_combine_kernel, lay=lay, f=f, steps=steps),
        grid=(steps,),
        in_specs=[
            pl.BlockSpec(memory_space=pl.ANY),
            pl.BlockSpec(memory_space=pl.ANY),
            pl.BlockSpec((rt, half), lambda i: (i, 0)),
            pl.BlockSpec((d, 2 * f), lambda i: (0, 0)),
            pl.BlockSpec((f, d), lambda i: (0, 0)),
            pl.BlockSpec((rt, d), lambda i: (i, 0)),
            pl.BlockSpec((1, 1, d), lambda i: (i * rt // lay.p, 0, gate_idx)),
            pl.BlockSpec((1, 1, d), lambda i: (lay.batch, 0, gate_idx)),
        ],
        out_specs=pl.BlockSpec((rt, d), lambda i: (i, 0)),
        out_shape=jax.ShapeDtypeStruct((lay.t, d), F32),
        scratch_shapes=[
            pltpu.VMEM((3, TOP_K * rt, half), U32),
            pltpu.SMEM((COMBINE_POS_RING * TOP_K * rt,), jnp.int32),
            pltpu.SemaphoreType.DMA((3,)),
            pltpu.SemaphoreType.DMA,
        ],
        input_output_aliases={5: 0},
        compiler_params=_params("arbitrary"),
        name="moe_combine",
    )(pos, y, hp, ws_in, ws_out, x, mods, mods)


def _final_norm_kernel(x_ref, w_ref, o_ref):
    x = x_ref[0]
    o_ref[0] = x * lax.rsqrt(jnp.mean(x * x, axis=-1, keepdims=True) + EPS) * w_ref[...]


def _final_norm(lay, x, w):
    d = lay.d
    rows = _pick(lay.seq, (256, 128, 64))
    assert lay.ctx_len % rows == 0
    off = lay.ctx_len // rows
    return pl.pallas_call(
        _final_norm_kernel,
        grid=(lay.batch, lay.seq // rows),
        in_specs=[pl.BlockSpec((1, rows, d), lambda b, j: (b, off + j, 0)),
                  pl.BlockSpec((1, d), lambda b, j: (0, 0))],
        out_specs=pl.BlockSpec((1, rows, d), lambda b, j: (b, j, 0)),
        out_shape=jax.ShapeDtypeStruct((lay.batch, lay.seq, d), F32),
        compiler_params=_params("arbitrary", "arbitrary"),
        name="final_norm",
    )(x.reshape(lay.batch, lay.p, d), w)


def kernel(x, c, ctx, c_ctx, ada_w_down, ada_w_up, ada_b, norm_mix, norm_ffn, conv_w_in, conv_w, conv_w_out, gla_w_in, gla_w_a2_fwd, gla_b_a2_fwd, gla_w_a2_bwd, gla_b_a2_bwd, gla_norm, gla_w_out, router_w, router_bias, exp_w_in, exp_w_out, shared_w_in, shared_w_out, norm_final):
    batch, seq, d = x.shape
    ctx_len = ctx.shape[1]
    depth = ada_w_down.shape[0]
    lay = _Layout(batch, ctx_len, seq, d)
    qk = d // 2
    tn = _pick(d, (1024, 512, 256, 128))

    cond = jnp.zeros((MOD_ROWS, d), F32).at[:batch].set(c).at[batch].set(c_ctx)
    mods_all = _ada_all(cond, ada_w_down, ada_w_up, ada_b)
    xs = jnp.concatenate([ctx, x], axis=1).reshape(lay.t, d)

    for i in range(depth):
        j = i // 2
        mods = mods_all[i].reshape(MOD_ROWS, 1, N_MOD * d)
        nw_mix = norm_mix[i].reshape(1, d)
        if i % 2 == 0:
            y = _norm_proj(lay, xs, nw_mix, mods, 0, 1, conv_w_in[j].astype(BF16), out_dtype=BF16,
                           tn=_pick(d, (256, 128)), conv_w=conv_w[j], name="conv_in")
            xs = _out_proj(lay, y, conv_w_out[j].astype(BF16), xs, mods, 2, tn=tn, name="conv_out")
        else:
            n_main = 2 * qk + 2 * d
            w_aux = jnp.zeros((d, LANES), BF16).at[:, :2 * GLA_GATE_RANK].set(
                gla_w_in[j][:, n_main:].astype(BF16))
            qkvr, a = _norm_proj(lay, xs, nw_mix, mods, 0, 1, gla_w_in[j].astype(BF16), out_dtype=BF16,
                                 tn=tn, n=n_main, w_aux=w_aux, name="gla_in")
            waf = jnp.zeros((LANES, qk), BF16).at[:GLA_GATE_RANK].set(gla_w_a2_fwd[j].astype(BF16))
            wab = jnp.zeros((LANES, qk), BF16).at[GLA_GATE_RANK:2 * GLA_GATE_RANK].set(
                gla_w_a2_bwd[j].astype(BF16))
            y = _gla(lay, qkvr, a, waf, gla_b_a2_fwd[j].reshape(1, qk), wab,
                     gla_b_a2_bwd[j].reshape(1, qk), gla_norm[j].reshape(1, d // GLA_HEADS))
            xs = _out_proj(lay, y, gla_w_out[j].astype(BF16), xs, mods, 2, tn=tn, name="gla_out")

        rw = jnp.zeros((d, LANES), BF16).at[:, :N_EXPERTS].set(router_w[i].astype(BF16))
        rb = jnp.zeros((1, LANES), F32).at[0, :N_EXPERTS].set(router_bias[i])
        hp, gates, ids = _router(lay, xs, norm_ffn[i].reshape(1, d), mods, rw, rb)
        tok, gate, block_expert, n_used, pos = _moe_plan(lay, gates, ids)
        y = _moe_experts(lay, i, hp, tok, gate, block_expert, n_used, exp_w_in, exp_w_out)
        xs = _moe_combine(lay, y, pos, hp, shared_w_in[i].astype(BF16), shared_w_out[i].astype(BF16),
                          xs, mods, 5)

    return _final_norm(lay, xs, norm_final.reshape(1, d))
```

```python
import functools

import jax
import jax.numpy as jnp
import numpy as np
from jax import lax
from jax.experimental import pallas as pl
from jax.experimental.pallas import tpu as pltpu

GRID_W = 64
EPS = 1e-6
N_MOD = 6
GLA_HEADS = 8
GLA_GATE_RANK = 16
GLA_GATE_TAU = 16.0
GLA_LOG_DECAY_FLOOR = -1.0
N_EXPERTS = 64
N_GROUPS = 8
TOPK_GROUPS = 4
TOP_K = 8
ROUTED_SCALE = 2.5

LANES = 128
GLA_ROWS = 128
TILES_PER_SAMPLE = 4
MOD_ROWS = 16
VMEM_LIMIT_BYTES = 60 * 1024 * 1024

MOE_ROWS = 256
COMBINE_ROWS = 128
MOE_EXTRA_BLOCKS = 5
MOE_ID_RING = 8
COMBINE_POS_RING = 4

BF16 = jnp.bfloat16
F32 = jnp.float32
U32 = jnp.uint32
HIGH_HALF = np.uint32(0xFFFF0000)
NEG_INF = float("-inf")


def _params(*sem):
    return pltpu.CompilerParams(dimension_semantics=sem, vmem_limit_bytes=VMEM_LIMIT_BYTES)


def _pick(n, prefs):
    for p in prefs:
        if n % p == 0:
            return p
    return n


def _ada_kernel(cond_ref, wd_ref, wu_ref, b_ref, o_ref, z_ref):
    @pl.when(pl.program_id(1) == 0)
    def _():
        s = jax.nn.silu(cond_ref[...]).astype(BF16)
        z_ref[...] = jnp.dot(s, wd_ref[0].astype(BF16), preferred_element_type=F32)

    o_ref[0] = jnp.dot(z_ref[...].astype(BF16), wu_ref[0].astype(BF16),
                       preferred_element_type=F32) + b_ref[0]


def _ada_all(cond, w_down, w_up, b):
    depth, d, r = w_down.shape
    n = w_up.shape[-1]
    tn = _pick(n, (2048, 1024, 512, 256, 128))
    return pl.pallas_call(
        _ada_kernel,
        grid=(depth, n // tn),
        in_specs=[
            pl.BlockSpec((MOD_ROWS, d), lambda l, j: (0, 0)),
            pl.BlockSpec((1, d, r), lambda l, j: (l, 0, 0)),
            pl.BlockSpec((1, r, tn), lambda l, j: (l, 0, j)),
            pl.BlockSpec((1, 1, tn), lambda l, j: (l, 0, j)),
        ],
        out_specs=pl.BlockSpec((1, MOD_ROWS, tn), lambda l, j: (l, 0, j)),
        out_shape=jax.ShapeDtypeStruct((depth, MOD_ROWS, n), F32),
        scratch_shapes=[pltpu.VMEM((MOD_ROWS, r), F32)],
        compiler_params=_params("arbitrary", "arbitrary"),
        name="ada_mod",
    )(cond, w_down, w_up, b.reshape(depth, 1, n))


class _Layout:
    def __init__(self, batch, ctx_len, seq, d):
        self.batch, self.ctx_len, self.seq, self.d = batch, ctx_len, seq, d
        self.p = ctx_len + seq
        self.t = batch * self.p
        assert self.p % TILES_PER_SAMPLE == 0
        self.tm = self.p // TILES_PER_SAMPLE
        assert GRID_W & (GRID_W - 1) == 0
        assert self.tm % GRID_W == 0 and ctx_len % GRID_W == 0 and self.tm % 16 == 0
        assert ctx_len <= self.tm and batch < MOD_ROWS
        self.n_tiles = self.t // self.tm
        self.slab = _pick(self.tm, (32, 16))
        assert ctx_len % self.slab == 0


def _mod_specs(lay, idx, two_axes=True):
    d = lay.d
    if two_axes:
        lat = pl.BlockSpec((1, 1, d), lambda i, j: (i // TILES_PER_SAMPLE, 0, idx))
        ctx = pl.BlockSpec((1, 1, d), lambda i, j: (lay.batch, 0, idx))
    else:
        lat = pl.BlockSpec((1, 1, d), lambda i: (i // TILES_PER_SAMPLE, 0, idx))
        ctx = pl.BlockSpec((1, 1, d), lambda i: (lay.batch, 0, idx))
    return lat, ctx


def _tile_row0(lay):
    return (pl.program_id(0) % TILES_PER_SAMPLE) * lay.tm


def _row_in_sample(lay, shape):
    return _tile_row0(lay) + lax.broadcasted_iota(jnp.int32, shape, 0)


def _pack_halves(x):
    half = x.shape[1] // 2
    xb = x.astype(BF16).astype(F32)
    lo = lax.bitcast_convert_type(xb[:, :half], U32) >> 16
    hi = lax.bitcast_convert_type(xb[:, half:], U32) & HIGH_HALF
    return hi | lo


def _unpack_halves(w):
    lo = lax.bitcast_convert_type(w << 16, F32)
    hi = lax.bitcast_convert_type(w & HIGH_HALF, F32)
    return lo, hi


def _modnorm_rows(lay, x_ref, nw_ref, shl_ref, scl_ref, shc_ref, scc_ref, h_ref, packed_ref=None):
    nw = nw_ref[...]
    slab = lay.slab
    row0 = _tile_row0(lay)

    def body(s, carry):
        r = pl.multiple_of(s * slab, slab)
        x = x_ref[pl.ds(r, slab), :]
        ms = jnp.mean(x * x, axis=-1, keepdims=True)
        y = x * lax.rsqrt(ms + EPS) * nw
        is_ctx = row0 + r < lay.ctx_len
        sc = jnp.where(is_ctx, scc_ref[0], scl_ref[0])
        sh = jnp.where(is_ctx, shc_ref[0], shl_ref[0])
        h = y * (1.0 + sc) + sh
        h_ref[pl.ds(r, slab), :] = h.astype(h_ref.dtype)
        if packed_ref is not None:
            packed_ref[pl.ds(r, slab), :] = _pack_halves(h)
        return carry

    lax.fori_loop(0, lay.tm // slab, body, 0)


def _proj_kernel(x_ref, nw_ref, shl, scl, shc, scc, w_ref, o_ref, h_ref, *, lay):
    @pl.when(pl.program_id(1) == 0)
    def _():
        _modnorm_rows(lay, x_ref, nw_ref, shl, scl, shc, scc, h_ref)

    o_ref[...] = jnp.dot(h_ref[...], w_ref[...], preferred_element_type=F32).astype(o_ref.dtype)


def _proj_aux_kernel(x_ref, nw_ref, shl, scl, shc, scc, w_ref, wa_ref, o_ref, a_ref, h_ref, *, lay):
    @pl.when(pl.program_id(1) == 0)
    def _():
        _modnorm_rows(lay, x_ref, nw_ref, shl, scl, shc, scc, h_ref)
        a_ref[...] = jnp.dot(h_ref[...], wa_ref[...], preferred_element_type=F32)

    o_ref[...] = jnp.dot(h_ref[...], w_ref[...], preferred_element_type=F32).astype(o_ref.dtype)


def _conv_proj_kernel(x_ref, nw_ref, shl, scl, shc, scc, wb_ref, wc_ref, wv_ref, cw_ref, o_ref, h_ref,
                      *, lay, tc):
    @pl.when(pl.program_id(1) == 0)
    def _():
        _modnorm_rows(lay, x_ref, nw_ref, shl, scl, shc, scc, h_ref)

    tm = lay.tm
    h = h_ref[...]
    bg = jnp.dot(h, wb_ref[...], preferred_element_type=F32)
    cg = jnp.dot(h, wc_ref[...], preferred_element_type=F32)
    v = jnp.dot(h, wv_ref[...], preferred_element_type=F32)
    u = cg * v
    row = _row_in_sample(lay, (tm, tc))
    is_ctx = row < lay.ctx_len
    col = (row - lay.ctx_len) & (GRID_W - 1)
    first = (is_ctx & (row == 0)) | (~is_ctx & (col == 0))
    last = (is_ctx & (row == lay.ctx_len - 1)) | (~is_ctx & (col == GRID_W - 1))
    prev = jnp.where(first, 0.0, pltpu.roll(u, 1, axis=0))
    nxt = jnp.where(last, 0.0, pltpu.roll(u, tm - 1, axis=0))
    cw = cw_ref[...]
    conv = prev * cw[0:1, :] + u * cw[1:2, :] + nxt * cw[2:3, :]
    o_ref[...] = (bg * conv).astype(o_ref.dtype)


def _norm_proj(lay, x, nw, mods, shift_idx, scale_idx, w, *, out_dtype, tn, n=None, w_aux=None,
               conv_w=None, name):
    d = w.shape[0]
    n = w.shape[1] if n is None else n
    assert n % tn == 0
    shl, shc = _mod_specs(lay, shift_idx)
    scl, scc = _mod_specs(lay, scale_idx)
    in_specs = [
        pl.BlockSpec((lay.tm, d), lambda i, j: (i, 0)),
        pl.BlockSpec((1, d), lambda i, j: (0, 0)),
        shl, scl, shc, scc,
        pl.BlockSpec((d, tn), lambda i, j: (0, j)),
    ]
    args = [x, nw, mods, mods, mods, mods, w]
    scratch = [pltpu.VMEM((lay.tm, d), BF16)]
    params = _params("arbitrary", "arbitrary")
    if conv_w is not None:
        tc = tn
        nct = n // 3 // tc
        in_specs[-1:] = [pl.BlockSpec((d, tc), functools.partial(lambda i, j, k: (0, k * nct + j), k=k))
                         for k in range(3)]
        in_specs.append(pl.BlockSpec((3, tc), lambda i, j: (0, j)))
        return pl.pallas_call(
            functools.partial(_conv_proj_kernel, lay=lay, tc=tc),
            grid=(lay.n_tiles, nct), in_specs=in_specs,
            out_specs=pl.BlockSpec((lay.tm, tc), lambda i, j: (i, j)),
            out_shape=jax.ShapeDtypeStruct((lay.t, n // 3), out_dtype),
            scratch_shapes=scratch, compiler_params=params, name=name,
        )(*args[:-1], w, w, w, conv_w)
    if w_aux is not None:
        na = w_aux.shape[1]
        in_specs.append(pl.BlockSpec((d, na), lambda i, j: (0, 0)))
        return pl.pallas_call(
            functools.partial(_proj_aux_kernel, lay=lay),
            grid=(lay.n_tiles, n // tn), in_specs=in_specs,
            out_specs=[pl.BlockSpec((lay.tm, tn), lambda i, j: (i, j)),
                       pl.BlockSpec((lay.tm, na), lambda i, j: (i, 0))],
            out_shape=[jax.ShapeDtypeStruct((lay.t, n), out_dtype),
                       jax.ShapeDtypeStruct((lay.t, na), F32)],
            scratch_shapes=scratch, compiler_params=params, name=name,
        )(*args, w_aux)
    return pl.pallas_call(
        functools.partial(_proj_kernel, lay=lay),
        grid=(lay.n_tiles, n // tn), in_specs=in_specs,
        out_specs=pl.BlockSpec((lay.tm, tn), lambda i, j: (i, j)),
        out_shape=jax.ShapeDtypeStruct((lay.t, n), out_dtype),
        scratch_shapes=scratch, compiler_params=params, name=name,
    )(*args)


def _out_proj_kernel(a_ref, w_ref, x_ref, gl_ref, gc_ref, o_ref, *, lay):
    y = jnp.dot(a_ref[...], w_ref[...], preferred_element_type=F32)
    is_ctx = _row_in_sample(lay, y.shape) < lay.ctx_len
    gate = jnp.where(is_ctx, gc_ref[0], gl_ref[0])
    o_ref[...] = x_ref[...] + gate * y


def _out_proj(lay, a, w, x, mods, gate_idx, *, tn, name):
    k, d = w.shape
    gl = pl.BlockSpec((1, 1, tn), lambda i, j: (i // TILES_PER_SAMPLE, 0, gate_idx * (d // tn) + j))
    gc = pl.BlockSpec((1, 1, tn), lambda i, j: (lay.batch, 0, gate_idx * (d // tn) + j))
    return pl.pallas_call(
        functools.partial(_out_proj_kernel, lay=lay),
        grid=(lay.n_tiles, d // tn),
        in_specs=[
            pl.BlockSpec((lay.tm, k), lambda i, j: (i, 0)),
            pl.BlockSpec((k, tn), lambda i, j: (0, j)),
            pl.BlockSpec((lay.tm, tn), lambda i, j: (i, j)),
            gl, gc,
        ],
        out_specs=pl.BlockSpec((lay.tm, tn), lambda i, j: (i, j)),
        out_shape=jax.ShapeDtypeStruct((lay.t, d), F32),
        input_output_aliases={2: 0},
        compiler_params=_params("arbitrary", "arbitrary"),
        name=name,
    )(a, w, x, mods, mods)


def _gla_kernel(q_ref, k_ref, v_ref, r_ref, a_ref, waf_ref, baf_ref, wab_ref, bab_ref, nw_ref,
                y_ref, of_ref, ob_ref, sf_ref, sb_ref, *, n_chunks, n_ctx_chunks, dk):
    c_rows = GLA_ROWS
    scale = dk ** -0.5
    ri = lax.broadcasted_iota(jnp.int32, (c_rows, c_rows), 0)
    ci = lax.broadcasted_iota(jnp.int32, (c_rows, c_rows), 1)
    nt = (((1,), (1,)), ((), ()))
    tn_dims = (((0,), (0,)), ((), ()))

    def chunk(c, forward, s_ref):
        rows = pl.ds(pl.multiple_of(c * c_rows, c_rows), c_rows)
        keep = (ci <= ri) if forward else (ci >= ri)
        tri = keep.astype(BF16)
        wa, ba = (waf_ref, baf_ref) if forward else (wab_ref, bab_ref)
        z = jnp.dot(a_ref[rows, :].astype(BF16), wa[...], preferred_element_type=F32) + ba[...]
        g = jnp.maximum(jax.nn.log_sigmoid(z) / GLA_GATE_TAU, GLA_LOG_DECAY_FLOOR)
        g1 = g.astype(BF16)
        rem = g - g1.astype(F32)
        g2 = rem.astype(BF16)
        g3 = (rem - g2.astype(F32)).astype(BF16)
        bcum = (jnp.dot(tri, g1, preferred_element_type=F32)
                + jnp.dot(tri, g2, preferred_element_type=F32)
                + jnp.dot(tri, g3, preferred_element_type=F32))
        mid = bcum[c_rows // 2:c_rows // 2 + 1, :]
        tot = bcum[c_rows - 1:c_rows, :] if forward else bcum[0:1, :]
        q = q_ref[rows, :].astype(F32) * scale
        k = k_ref[rows, :].astype(F32)
        v = v_ref[rows, :]
        q_rel = (q * jnp.exp(bcum - mid)).astype(BF16)
        k_rel = (k * jnp.exp(mid - bcum)).astype(BF16)
        q_dec = (q * jnp.exp(bcum)).astype(BF16)
        k_end = (k * jnp.exp(tot - bcum)).astype(BF16)
        att = lax.dot_general(q_rel, k_rel, nt, preferred_element_type=F32)
        att = jnp.where(keep, att, 0.0).astype(BF16)
        s = s_ref[...]
        o = (jnp.dot(att, v, preferred_element_type=F32)
             + lax.dot_general(q_dec, s.astype(BF16), nt, preferred_element_type=F32))
        s_ref[...] = jnp.exp(tot) * s + lax.dot_general(v, k_end, tn_dims, preferred_element_type=F32)
        return rows, o

    sf_ref[...] = jnp.zeros_like(sf_ref)
    sb_ref[...] = jnp.zeros_like(sb_ref)

    def scan_body(n, carry):
        rows, o = chunk(n, True, sf_ref)
        of_ref[rows, :] = o
        c = jnp.where(n < n_ctx_chunks, n_ctx_chunks - 1 - n, n_chunks - 1 - (n - n_ctx_chunks))
        rows, o = chunk(c, False, sb_ref)
        ob_ref[rows, :] = o
        return carry

    lax.fori_loop(0, n_chunks, scan_body, 0)
    nw = nw_ref[...]

    def out_body(n, carry):
        rows = pl.ds(pl.multiple_of(n * c_rows, c_rows), c_rows)
        o = of_ref[rows, :] + ob_ref[rows, :]
        o = o * lax.rsqrt(jnp.mean(o * o, axis=-1, keepdims=True) + EPS) * nw
        y_ref[rows, :] = (o * jax.nn.silu(r_ref[rows, :].astype(F32))).astype(y_ref.dtype)
        return carry

    lax.fori_loop(0, n_chunks, out_body, 0)


def _gla(lay, qkvr, a, waf, baf, wab, bab, norm_w):
    d = lay.d
    dk = d // 2 // GLA_HEADS
    dv = d // GLA_HEADS
    assert lay.ctx_len % GLA_ROWS == 0 and lay.seq % GLA_ROWS == 0
    n_chunks = lay.p // GLA_ROWS
    kern = functools.partial(_gla_kernel, n_chunks=n_chunks,
                             n_ctx_chunks=lay.ctx_len // GLA_ROWS, dk=dk)
    h = GLA_HEADS
    return pl.pallas_call(
        kern,
        grid=(lay.batch, h),
        in_specs=[
            pl.BlockSpec((lay.p, dk), lambda b, i: (b, i)),
            pl.BlockSpec((lay.p, dk), lambda b, i: (b, h + i)),
            pl.BlockSpec((lay.p, dv), lambda b, i: (b, h + i)),
            pl.BlockSpec((lay.p, dv), lambda b, i: (b, 2 * h + i)),
            pl.BlockSpec((lay.p, LANES), lambda b, i: (b, 0)),
            pl.BlockSpec((LANES, dk), lambda b, i: (0, i)),
            pl.BlockSpec((1, dk), lambda b, i: (0, i)),
            pl.BlockSpec((LANES, dk), lambda b, i: (0, i)),
            pl.BlockSpec((1, dk), lambda b, i: (0, i)),
            pl.BlockSpec((1, dv), lambda b, i: (0, 0)),
        ],
        out_specs=pl.BlockSpec((lay.p, dv), lambda b, i: (b, i)),
        out_shape=jax.ShapeDtypeStruct((lay.t, d), BF16),
        scratch_shapes=[pltpu.VMEM((lay.p, dv), F32), pltpu.VMEM((lay.p, dv), F32),
                        pltpu.VMEM((dv, dk), F32), pltpu.VMEM((dv, dk), F32)],
        compiler_params=_params("arbitrary", "arbitrary"),
        name="gla_scan",
    )(qkvr, qkvr, qkvr, qkvr, a, waf, baf, wab, bab, norm_w)


def _route(logits, bias):
    n_rows = logits.shape[0]
    per_group = N_EXPERTS // N_GROUPS
    lane_i = lax.broadcasted_iota(jnp.int32, (n_rows, LANES), 1)
    lane = lane_i.astype(F32)
    group = (lane_i // per_group).astype(F32)
    valid = lane_i < N_EXPERTS
    big = float(2 * LANES)
    scores = jax.nn.sigmoid(logits)
    sel = jnp.where(valid, scores + bias, NEG_INF)

    def partner(x, s):
        return jnp.where((lane_i & s) == 0, pltpu.roll(x, LANES - s, axis=1), pltpu.roll(x, s, axis=1))

    def group_reduce(x, op):
        s = 1
        while s < per_group:
            x = op(x, partner(x, s))
            s *= 2
        return x

    m1 = group_reduce(sel, jnp.maximum)
    first = group_reduce(jnp.where(sel == m1, lane, big), jnp.minimum)
    m2 = group_reduce(jnp.where(lane == first, NEG_INF, sel), jnp.maximum)
    cur = jnp.where(valid, m1 + m2, NEG_INF)

    def pick_best(cur, ident):
        m = jnp.max(cur, axis=1, keepdims=True)
        best = jnp.min(jnp.where(cur == m, ident, big), axis=1, keepdims=True)
        return ident == best

    group_ok = jnp.zeros((n_rows, LANES), jnp.bool_)
    for _ in range(TOPK_GROUPS):
        p = pick_best(cur, group)
        group_ok = group_ok | p
        cur = jnp.where(p, NEG_INF, cur)

    cur = jnp.where(group_ok & valid, sel, NEG_INF)
    picked = jnp.zeros((n_rows, LANES), jnp.bool_)
    ids = jnp.zeros((n_rows, LANES), F32)
    for it in range(TOP_K):
        m = jnp.max(cur, axis=1, keepdims=True)
        best = jnp.min(jnp.where(cur == m, lane, big), axis=1, keepdims=True)
        p = lane == best
        picked = picked | p
        cur = jnp.where(p, NEG_INF, cur)
        ids = jnp.where(lane_i == it, best, ids)

    w = jnp.where(picked, scores, 0.0)
    gates = w / jnp.sum(w, axis=1, keepdims=True) * ROUTED_SCALE
    return gates, ids.astype(jnp.int32)


def _router_kernel(x_ref, nw_ref, shl, scl, shc, scc, rw_ref, rb_ref, hp_ref, g_ref, id_ref, h_ref, *, lay):
    _modnorm_rows(lay, x_ref, nw_ref, shl, scl, shc, scc, h_ref, packed_ref=hp_ref)
    logits = jnp.dot(h_ref[...], rw_ref[...], preferred_element_type=F32)
    g_ref[...], id_ref[...] = _route(logits, rb_ref[...])


def _router(lay, x, nw, mods, rw, rb):
    d = lay.d
    shl, shc = _mod_specs(lay, 3, two_axes=False)
    scl, scc = _mod_specs(lay, 4, two_axes=False)
    return pl.pallas_call(
        functools.partial(_router_kernel, lay=lay),
        grid=(lay.n_tiles,),
        in_specs=[
            pl.BlockSpec((lay.tm, d), lambda i: (i, 0)),
            pl.BlockSpec((1, d), lambda i: (0, 0)),
            shl, scl, shc, scc,
            pl.BlockSpec((d, LANES), lambda i: (0, 0)),
            pl.BlockSpec((1, LANES), lambda i: (0, 0)),
        ],
        out_specs=[pl.BlockSpec((lay.tm, d // 2), lambda i: (i, 0)),
                   pl.BlockSpec((lay.tm, LANES), lambda i: (i, 0)),
                   pl.BlockSpec((lay.tm, LANES), lambda i: (i, 0))],
        out_shape=[jax.ShapeDtypeStruct((lay.t, d // 2), U32),
                   jax.ShapeDtypeStruct((lay.t, LANES), F32),
                   jax.ShapeDtypeStruct((lay.t, LANES), jnp.int32)],
        scratch_shapes=[pltpu.VMEM((lay.tm, d), BF16)],
        compiler_params=_params("arbitrary"),
        name="moe_router",
    )(x, nw, mods, mods, mods, mods, rw, rb)


def _moe_plan(lay, gates, ids):
    t, rb = lay.t, MOE_ROWS
    n_pairs = t * TOP_K
    n_blocks = n_pairs // rb + N_EXPERTS
    assert N_EXPERTS * n_pairs < 2 ** 31
    eidx = ids[:, :TOP_K]
    w = jnp.take_along_axis(gates, eidx, axis=1).reshape(-1)
    key = eidx.reshape(-1) * n_pairs + jnp.arange(n_pairs, dtype=jnp.int32)
    key_s, w_s = lax.sort((key, w), num_keys=1)
    bounds = jnp.searchsorted(key_s, jnp.arange(N_EXPERTS + 1, dtype=jnp.int32) * n_pairs).astype(jnp.int32)
    start, counts = bounds[:-1], bounds[1:] - bounds[:-1]
    padded = (counts + rb - 1) // rb * rb
    pad_end = jnp.cumsum(padded)
    pad_start = pad_end - padded
    blk = jnp.arange(n_blocks + MOE_EXTRA_BLOCKS, dtype=jnp.int32) - 1
    e_blk = jnp.clip(jnp.searchsorted(pad_end, blk * rb, side="right", method="compare_all"),
                     0, N_EXPERTS - 1)
    r = jnp.arange(rb, dtype=jnp.int32)[None, :]
    in_run = (blk * rb - pad_start[e_blk])[:, None] + r
    valid = ((blk >= 0) & (blk * rb < pad_end[-1]))[:, None] & (in_run < counts[e_blk][:, None])
    src = jnp.clip(start[e_blk][:, None] + in_run, 0, n_pairs - 1)
    pair = key_s[src] % n_pairs
    tok = jnp.where(valid, pair // TOP_K, r)
    gate = jnp.where(valid, w_s[src], 0.0)[1:n_blocks + 1].reshape(n_blocks * rb, 1)
    n_used = pad_end[-1] // rb
    step_blk = jnp.minimum(jnp.arange(n_blocks + 1, dtype=jnp.int32), n_used - 1)
    block_expert = jnp.searchsorted(pad_end, step_blk * rb, side="right",
                                    method="compare_all").astype(jnp.int32)
    chose = jnp.sum(eidx[:, :, None] == jnp.arange(N_EXPERTS, dtype=jnp.int32)[None, None, :], axis=1,
                    dtype=jnp.int32)
    earlier = jnp.cumsum(chose, axis=0) - chose
    pos = pad_start[eidx] + jnp.take_along_axis(earlier, eidx, axis=1)
    rt = COMBINE_ROWS
    pos = pos.reshape(t // rt, rt, TOP_K).transpose(0, 2, 1).reshape(t // rt, TOP_K * rt)
    pos = jnp.concatenate([pos, jnp.zeros((2, TOP_K * rt), pos.dtype)]).astype(jnp.int32)
    return tok.astype(jnp.int32), gate, block_expert, n_used.reshape(1).astype(jnp.int32), pos


def _moe_expert_kernel(be_ref, nu_ref, tok_hbm, g_ref, win_ref, wout_ref, h_hbm, y_ref,
                       hbuf, winb, woutb, tok_smem, gsem, isem, *, f):
    nb = pl.program_id(0)
    pl.when(nb <= nu_ref[0])(functools.partial(
        _moe_expert_step, nb, nu_ref[0], be_ref, tok_hbm, g_ref, win_ref, wout_ref, h_hbm, y_ref,
        hbuf, winb, woutb, tok_smem, gsem, isem, f=f))


def _moe_expert_step(nb, n_used, be_ref, tok_hbm, g_ref, win_ref, wout_ref, h_hbm, y_ref,
                     hbuf, winb, woutb, tok_smem, gsem, isem, *, f):
    rb = MOE_ROWS
    hs = nb % 3
    half = hbuf.shape[-1]

    def ring(block):
        return pl.multiple_of(((block + 1) % MOE_ID_RING) * rb, rb)

    def ids_copy(block):
        return pltpu.make_async_copy(tok_hbm.at[block + 1], tok_smem.at[pl.ds(ring(block), rb)], isem)

    def gather_row(k, r, s):
        return pltpu.make_async_copy(h_hbm.at[pl.ds(tok_smem[k + r], 1)], hbuf.at[s, pl.ds(r, 1)],
                                     gsem.at[s])

    def gather_wait(s):
        pltpu.make_async_copy(h_hbm.at[pl.ds(0, rb)], hbuf.at[s], gsem.at[s]).wait()

    @pl.when(nb == 0)
    def _():
        for block in (0, 1, 2):
            c = ids_copy(block)
            c.start()
            c.wait()
        for block in (0, 1):
            def body(r, carry, block=block):
                gather_row((block + 1) * rb, r, block).start()
                return carry
            lax.fori_loop(0, rb, body, 0)

    @pl.when(nb > 0)
    def _():
        ids_copy(nb + 2).wait()

    ids_copy(nb + 3).start()

    @pl.when((nb == 0) | (be_ref[nb] != be_ref[jnp.maximum(nb - 1, 0)]))
    def _():
        winb[...] = win_ref[0, 0].astype(BF16)
        woutb[...] = wout_ref[0, 0].astype(BF16)

    gk = ring(nb + 2)
    ghs = (nb + 2) % 3
    n_groups = 8
    per_group = rb // n_groups

    def issue(group):
        for r in range(group * per_group, (group + 1) * per_group):
            gather_row(gk, r, ghs).start(priority=r % 2)

    gather_wait(hs)
    issue(0)
    cols = half // 4
    hu = jnp.zeros((rb, 2 * f), F32)
    for piece in range(4):
        c0 = piece * cols
        lo, hi = _unpack_halves(hbuf[hs, :, c0:c0 + cols])
        hu = (hu + jnp.dot(lo.astype(BF16), winb[c0:c0 + cols, :], preferred_element_type=F32)
              + jnp.dot(hi.astype(BF16), winb[half + c0:half + c0 + cols, :],
                        preferred_element_type=F32))
        issue(1 + piece)
    act = (jax.nn.silu(hu[:, :f]) * hu[:, f:] * g_ref[...]).astype(BF16)
    for piece in range(4):
        c0 = piece * cols
        y_lo = jnp.dot(act, woutb[:, c0:c0 + cols], preferred_element_type=F32)
        y_hi = jnp.dot(act, woutb[:, half + c0:half + c0 + cols], preferred_element_type=F32)
        y_ref[:, c0:c0 + cols] = _pack_halves(jnp.concatenate([y_lo, y_hi], axis=1))
        if piece < 3:
            issue(5 + piece)

    @pl.when(nb == n_used)
    def _():
        gather_wait((nb + 1) % 3)
        gather_wait((nb + 2) % 3)
        ids_copy(nb + 3).wait()


def _moe_experts(lay, layer, hp, tok, gate, block_expert, n_used, w_in, w_out):
    _, ne, d, f2 = w_in.shape
    f = f2 // 2
    rb = MOE_ROWS
    half = d // 2
    n_blocks = gate.shape[0] // rb
    assert tok.shape == (n_blocks + MOE_EXTRA_BLOCKS, rb) and rb <= lay.t
    kern = functools.partial(_moe_expert_kernel, f=f)
    last = n_blocks - 1
    grid_spec = pltpu.PrefetchScalarGridSpec(
        num_scalar_prefetch=2,
        grid=(n_blocks + 1,),
        in_specs=[
            pl.BlockSpec(memory_space=pl.ANY),
            pl.BlockSpec((rb, 1), lambda nb, be, nu: (jnp.minimum(nb, last), 0)),
            pl.BlockSpec((1, 1, d, f2), lambda nb, be, nu: (layer, be[nb], 0, 0)),
            pl.BlockSpec((1, 1, f, d), lambda nb, be, nu: (layer, be[nb], 0, 0)),
            pl.BlockSpec(memory_space=pl.ANY),
        ],
        out_specs=pl.BlockSpec((rb, half), lambda nb, be, nu: (jnp.minimum(nb, nu[0]), 0)),
        scratch_shapes=[
            pltpu.VMEM((3, rb, half), U32),
            pltpu.VMEM((d, f2), BF16),
            pltpu.VMEM((f, d), BF16),
            pltpu.SMEM((MOE_ID_RING * rb,), jnp.int32),
            pltpu.SemaphoreType.DMA((3,)),
            pltpu.SemaphoreType.DMA,
        ],
    )
    return pl.pallas_call(
        kern,
        grid_spec=grid_spec,
        out_shape=jax.ShapeDtypeStruct(((n_blocks + 1) * rb, half), U32),
        compiler_params=_params("arbitrary"),
        name="moe_experts",
    )(block_expert, n_used, tok, gate, w_in, w_out, hp)


def _moe_combine_kernel(pos_hbm, y_hbm, h_ref, wsi_ref, wso_ref, x_ref, gl_ref, gc_ref, o_ref,
                        ybuf, pos_smem, gsem, isem, *, lay, f, steps):
    rt = COMBINE_ROWS
    n_rows = TOP_K * rt
    i = pl.program_id(0)
    s = i % 2
    half = h_ref.shape[-1]

    def ring(step):
        return pl.multiple_of((step % COMBINE_POS_RING) * n_rows, n_rows)

    def pos_copy(step):
        return pltpu.make_async_copy(pos_hbm.at[step], pos_smem.at[pl.ds(ring(step), n_rows)], isem)

    def gather_row(k, r, slot):
        return pltpu.make_async_copy(y_hbm.at[pl.ds(pos_smem[k + r], 1)], ybuf.at[slot, pl.ds(r, 1)],
                                     gsem.at[slot])

    def gather_wait(slot):
        pltpu.make_async_copy(y_hbm.at[pl.ds(0, n_rows)], ybuf.at[slot], gsem.at[slot]).wait()

    @pl.when(i == 0)
    def _():
        for step in (0, 1):
            c = pos_copy(step)
            c.start()
            c.wait()

        def body(r, carry):
            gather_row(0, r, 0).start()
            return carry
        lax.fori_loop(0, n_rows, body, 0)

    @pl.when(i > 0)
    def _():
        pos_copy(i + 1).wait()

    pos_copy(i + 2).start()
    nk = ring(i + 1)
    gather_wait(s)
    lo = hi = None
    for k in range(TOP_K):
        for r in range(k * rt, (k + 1) * rt):
            gather_row(nk, r, 1 - s).start(priority=r % 2)
        l, h = _unpack_halves(ybuf[s, k * rt:(k + 1) * rt, :])
        lo, hi = (l, h) if lo is None else (lo + l, hi + h)
    hlo, hhi = _unpack_halves(h_ref[...])
    hu = (jnp.dot(hlo.astype(BF16), wsi_ref[:half, :], preferred_element_type=F32)
          + jnp.dot(hhi.astype(BF16), wsi_ref[half:, :], preferred_element_type=F32))
    act = (jax.nn.silu(hu[:, :f]) * hu[:, f:]).astype(BF16)
    shared = jnp.dot(act, wso_ref[...], preferred_element_type=F32)
    is_ctx = (pl.program_id(0) * COMBINE_ROWS) % lay.p < lay.ctx_len
    gate = jnp.where(is_ctx, gc_ref[0], gl_ref[0])
    o_ref[:, :half] = x_ref[:, :half] + gate[:, :half] * (lo + shared[:, :half])
    o_ref[:, half:] = x_ref[:, half:] + gate[:, half:] * (hi + shared[:, half:])

    @pl.when(i == steps - 1)
    def _():
        gather_wait(1 - s)
        pos_copy(i + 2).wait()


def _moe_combine(lay, y, pos, hp, ws_in, ws_out, x, mods, gate_idx):
    d = lay.d
    half = d // 2
    rt = COMBINE_ROWS
    assert lay.ctx_len % rt == 0 and lay.p % rt == 0
    steps = lay.t // rt
    assert pos.shape == (steps + 2, TOP_K * rt)
    f = ws_out.shape[0]
    return pl.pallas_call(
        functools.partial(_moe_combine_kernel, lay=lay, f=f, steps=steps),
        grid=(steps,),
        in_specs=[
            pl.BlockSpec(memory_space=pl.ANY),
            pl.BlockSpec(memory_space=pl.ANY),
            pl.BlockSpec((rt, half), lambda i: (i, 0)),
            pl.BlockSpec((d, 2 * f), lambda i: (0, 0)),
            pl.BlockSpec((f, d), lambda i: (0, 0)),
            pl.BlockSpec((rt, d), lambda i: (i, 0)),
            pl.BlockSpec((1, 1, d), lambda i: (i * rt // lay.p, 0, gate_idx)),
            pl.BlockSpec((1, 1, d), lambda i: (lay.batch, 0, gate_idx)),
        ],
        out_specs=pl.BlockSpec((rt, d), lambda i: (i, 0)),
        out_shape=jax.ShapeDtypeStruct((lay.t, d), F32),
        scratch_shapes=[
            pltpu.VMEM((2, TOP_K * rt, half), U32),
            pltpu.SMEM((COMBINE_POS_RING * TOP_K * rt,), jnp.int32),
            pltpu.SemaphoreType.DMA((2,)),
            pltpu.SemaphoreType.DMA,
        ],
        input_output_aliases={5: 0},
        compiler_params=_params("arbitrary"),
        name="moe_combine",
    )(pos, y, hp, ws_in, ws_out, x, mods, mods)


def _final_norm_kernel(x_ref, w_ref, o_ref):
    x = x_ref[0]
    o_ref[0] = x * lax.rsqrt(jnp.mean(x * x, axis=-1, keepdims=True) + EPS) * w_ref[...]


def _final_norm(lay, x, w):
    d = lay.d
    rows = _pick(lay.seq, (256, 128, 64))
    assert lay.ctx_len % rows == 0
    off = lay.ctx_len // rows
    return pl.pallas_call(
        _final_norm_kernel,
        grid=(lay.batch, lay.seq // rows),
        in_specs=[pl.BlockSpec((1, rows, d), lambda b, j: (b, off + j, 0)),
                  pl.BlockSpec((1, d), lambda b, j: (0, 0))],
        out_specs=pl.BlockSpec((1, rows, d), lambda b, j: (b, j, 0)),
        out_shape=jax.ShapeDtypeStruct((lay.batch, lay.seq, d), F32),
        compiler_params=_params("arbitrary", "arbitrary"),
        name="final_norm",
    )(x.reshape(lay.batch, lay.p, d), w)


def kernel(x, c, ctx, c_ctx, ada_w_down, ada_w_up, ada_b, norm_mix, norm_ffn, conv_w_in, conv_w, conv_w_out, gla_w_in, gla_w_a2_fwd, gla_b_a2_fwd, gla_w_a2_bwd, gla_b_a2_bwd, gla_norm, gla_w_out, router_w, router_bias, exp_w_in, exp_w_out, shared_w_in, shared_w_out, norm_final):
    batch, seq, d = x.shape
    ctx_len = ctx.shape[1]
    depth = ada_w_down.shape[0]
    lay = _Layout(batch, ctx_len, seq, d)
    qk = d // 2
    tn = _pick(d, (1024, 512, 256, 128))

    cond = jnp.zeros((MOD_ROWS, d), F32).at[:batch].set(c).at[batch].set(c_ctx)
    mods_all = _ada_all(cond, ada_w_down, ada_w_up, ada_b)
    xs = jnp.concatenate([ctx, x], axis=1).reshape(lay.t, d)

    for i in range(depth):
        j = i // 2
        mods = mods_all[i].reshape(MOD_ROWS, 1, N_MOD * d)
        nw_mix = norm_mix[i].reshape(1, d)
        if i % 2 == 0:
            y = _norm_proj(lay, xs, nw_mix, mods, 0, 1, conv_w_in[j].astype(BF16), out_dtype=BF16,
                           tn=_pick(d, (512, 256, 128)), conv_w=conv_w[j], name="conv_in")
            xs = _out_proj(lay, y, conv_w_out[j].astype(BF16), xs, mods, 2, tn=tn, name="conv_out")
        else:
            n_main = 2 * qk + 2 * d
            w_aux = jnp.zeros((d, LANES), BF16).at[:, :2 * GLA_GATE_RANK].set(
                gla_w_in[j][:, n_main:].astype(BF16))
            qkvr, a = _norm_proj(lay, xs, nw_mix, mods, 0, 1, gla_w_in[j].astype(BF16), out_dtype=BF16,
                                 tn=tn, n=n_main, w_aux=w_aux, name="gla_in")
            waf = jnp.zeros((LANES, qk), BF16).at[:GLA_GATE_RANK].set(gla_w_a2_fwd[j].astype(BF16))
            wab = jnp.zeros((LANES, qk), BF16).at[GLA_GATE_RANK:2 * GLA_GATE_RANK].set(
                gla_w_a2_bwd[j].astype(BF16))
            y = _gla(lay, qkvr, a, waf, gla_b_a2_fwd[j].reshape(1, qk), wab,
                     gla_b_a2_bwd[j].reshape(1, qk), gla_norm[j].reshape(1, d // GLA_HEADS))
            xs = _out_proj(lay, y, gla_w_out[j].astype(BF16), xs, mods, 2, tn=tn, name="gla_out")

        rw = jnp.zeros((d, LANES), BF16).at[:, :N_EXPERTS].set(router_w[i].astype(BF16))
        rb = jnp.zeros((1, LANES), F32).at[0, :N_EXPERTS].set(router_bias[i])
        hp, gates, ids = _router(lay, xs, norm_ffn[i].reshape(1, d), mods, rw, rb)
        tok, gate, block_expert, n_used, pos = _moe_plan(lay, gates, ids)
        y = _moe_experts(lay, i, hp, tok, gate, block_expert, n_used, exp_w_in, exp_w_out)
        xs = _moe_combine(lay, y, pos, hp, shared_w_in[i].astype(BF16), shared_w_out[i].astype(BF16),
                          xs, mods, 5)

    return _final_norm(lay, xs, norm_final.reshape(1, d))
```
